```python
import math
import jax, jax.numpy as jnp
from jax import lax
import numpy as np

D_MODEL = 1024
BATCH = 32
SEQ = 2048
DEPTH = 4
DEC_BATCH = 4
DEC_SEQ = 8192
PAST_LEN = 128

GRID_W = 64
D_MIX = D_MODEL
HG_HEADS = 4
HG_DK = 128
HG_DV = 128
HG_KEY = HG_HEADS * HG_DK
HG_WIDTH = HG_HEADS * HG_DV
CHUNK = 32
NA_HEADS = 8
NA_DH = 64
NA_WIDTH = NA_HEADS * NA_DH
WIN_H = 8
WIN_W = 16
QBLK_W = 16
BAND_W = QBLK_W + WIN_W
IN_COLS = 2 * HG_KEY + HG_KEY + 2 * HG_WIDTH + 3 * NA_WIDTH
FF_DENSE = 2816
N_EXPERTS = 8
TOP_K = 2
FF_EXPERT = 3584
N_DENSE = (DEPTH + 1) // 2
N_MOE = DEPTH // 2
EPS = 1e-6

kernel_name = "hymba_hgrn2_natten_encoder"


def rmsnorm(x, w):
    xf = x.astype(jnp.float32)
    y = xf * lax.rsqrt(jnp.mean(xf * xf, axis=-1, keepdims=True) + EPS) * w.astype(jnp.float32)
    return y.astype(x.dtype)


def swiglu(h, wg, wu, wd):
    return (jax.nn.silu(h @ wg) * (h @ wu)) @ wd


def gla_chunk_scan(q, k, v, logf):
    B, T, H, dk = q.shape
    dv = v.shape[-1]
    n_chunks = T // CHUNK

    def to_chunks(a):
        return a.reshape(B, n_chunks, CHUNK, H, a.shape[-1]).transpose(1, 0, 3, 2, 4)

    mask = jnp.tril(jnp.ones((CHUNK, CHUNK), dtype=bool))

    def step(S, inp):
        qc, kc, vc, gc = inp
        b = jnp.cumsum(gc, axis=2)
        b_last = b[:, :, -1:, :]
        o_inter = jnp.einsum('bhcd,bhde->bhce', qc * jnp.exp(b), S)
        diff = b[:, :, :, None, :] - b[:, :, None, :, :]
        decay = jnp.exp(jnp.where(mask[None, None, :, :, None], diff, -jnp.inf))
        A = jnp.einsum('bhid,bhijd->bhij', qc, decay * kc[:, :, None, :, :])
        o_intra = jnp.einsum('bhij,bhje->bhie', A, vc)
        S_new = jnp.exp(b_last)[:, :, 0, :, None] * S + jnp.einsum(
            'bhcd,bhce->bhde', kc * jnp.exp(b_last - b), vc)
        return S_new, o_inter + o_intra

    S0 = jnp.zeros((B, H, dk, dv), jnp.float32)
    _, o = lax.scan(step, S0, (to_chunks(q), to_chunks(k), to_chunks(v), to_chunks(logf)))
    return o.transpose(1, 0, 3, 2, 4).reshape(B, T, H, dv)


def hgrn2_forget(f_pre, lb):
    lbh = lb.reshape(HG_HEADS, HG_DK)
    return jnp.logaddexp(jnp.log(lbh), jnp.log1p(-lbh) + jax.nn.log_sigmoid(f_pre))


def hgrn2_mixer(q, f_fwd, f_bwd, i_in, g, lb_fwd, lb_bwd, gnorm_w):
    B, T, _ = q.shape

    def heads(a, d):
        return a.reshape(B, T, HG_HEADS, d).astype(jnp.float32)

    qh = jax.nn.silu(heads(q, HG_DK))
    vh = heads(i_in, HG_DV)
    logf_f = hgrn2_forget(heads(f_fwd, HG_DK), lb_fwd)
    o_f = gla_chunk_scan(qh, -jnp.expm1(logf_f), vh, logf_f)
    logf_b = hgrn2_forget(heads(f_bwd, HG_DK), lb_bwd)[:, ::-1]
    o_b = gla_chunk_scan(qh[:, ::-1], -jnp.expm1(logf_b), vh[:, ::-1], logf_b)[:, ::-1]
    o = rmsnorm(o_f + o_b, gnorm_w) * jax.nn.silu(heads(g, HG_DV))
    return o.reshape(B, T, HG_WIDTH).astype(q.dtype)


def neighborhood_attention(q, k, v, qn_w, kn_w, rpb):
    B, T, _ = q.shape
    R = T // GRID_W
    KH = min(WIN_H, R)
    n_cb = GRID_W // QBLK_W
    shp = (B, R, GRID_W, NA_HEADS, NA_DH)
    qg = rmsnorm(q.reshape(shp), qn_w)
    kg = rmsnorm(k.reshape(shp), kn_w)
    vg = v.reshape(shp)
    qcol = np.arange(GRID_W).reshape(n_cb, QBLK_W)
    band_start = np.clip(np.arange(n_cb) * QBLK_W - WIN_W // 2, 0, GRID_W - BAND_W)
    kcol = band_start[:, None] + np.arange(BAND_W)
    cstart = np.clip(qcol - WIN_W // 2, 0, GRID_W - WIN_W)
    kc3 = kcol[:, None, :]
    col_mask = jnp.asarray((kc3 >= cstart[..., None]) & (kc3 < cstart[..., None] + WIN_W))
    dx_idx = np.clip(kc3 - qcol[..., None] + WIN_W - 1, 0, 2 * WIN_W - 2)
    rpb_col = rpb.astype(jnp.float32)[:, :, dx_idx]
    k_band = kg[:, :, kcol]
    v_band = vg[:, :, kcol]
    scale = NA_DH ** -0.5

    def row_block(args):
        r, q_row = args
        rs = jnp.clip(r - KH // 2, 0, R - KH)
        kb = lax.dynamic_slice_in_dim(k_band, rs, KH, axis=1)
        vb = lax.dynamic_slice_in_dim(v_band, rs, KH, axis=1)
        s = jnp.einsum('bnqhd,bknjhd->bhnqkj', q_row, kb,
                       preferred_element_type=jnp.float32) * scale
        dy_idx = rs + jnp.arange(KH) - r + WIN_H - 1
        bias = jnp.take(rpb_col, dy_idx, axis=1).transpose(0, 2, 3, 1, 4)
        s = jnp.where(col_mask[None, None, :, :, None, :], s + bias[None], -jnp.inf)
        p = jax.nn.softmax(s.reshape(B, NA_HEADS, n_cb, QBLK_W, KH * BAND_W), axis=-1)
        p = p.reshape(s.shape).astype(vb.dtype)
        o = jnp.einsum('bhnqkj,bknjhd->bnqhd', p, vb)
        return o.reshape(B, GRID_W, NA_WIDTH)

    q_rows = qg.reshape(B, R, n_cb, QBLK_W, NA_HEADS, NA_DH).transpose(1, 0, 2, 3, 4, 5)
    out = lax.map(row_block, (jnp.arange(R), q_rows))
    return out.transpose(1, 0, 2, 3).reshape(B, T, NA_WIDTH)


def moe_swiglu(h, w_router, w_gate, w_up, w_down):
    logits = jnp.einsum('btd,de->bte', h, w_router).astype(jnp.float32)
    top_v, top_i = lax.top_k(logits, TOP_K)
    top_w = jax.nn.softmax(top_v, axis=-1)
    gates = jnp.sum(jax.nn.one_hot(top_i, N_EXPERTS, dtype=jnp.float32) * top_w[..., None], axis=-2)
    gates = gates.astype(h.dtype)
    out = jnp.zeros_like(h)
    for e in range(N_EXPERTS):
        out = out + gates[..., e:e + 1] * swiglu(h, w_gate[e], w_up[e], w_down[e])
    return out


def trunk(x, norm_mix, w_in, lb_all, hg_gnorm, na_qnorm, na_knorm, na_rpb, w_out, norm_ffn,
          ffn_gate, ffn_up, ffn_down, router, exp_gate, exp_up, exp_down):
    s = np.cumsum([0, HG_KEY, HG_KEY, HG_KEY, HG_WIDTH, HG_WIDTH, NA_WIDTH, NA_WIDTH, NA_WIDTH])
    for l in range(DEPTH):
        h = rmsnorm(x, norm_mix[l])
        u = h @ w_in[l]
        q_hg, f_f, f_b, i_in, g, q_na, k_na, v_na = [u[..., s[j]:s[j + 1]] for j in range(8)]
        o_hg = hgrn2_mixer(q_hg, f_f, f_b, i_in, g, lb_all[0, l], lb_all[1, l], hg_gnorm[l])
        o_na = neighborhood_attention(q_na, k_na, v_na, na_qnorm[l], na_knorm[l], na_rpb[l])
        x = x + jnp.concatenate([o_hg, o_na.astype(x.dtype)], axis=-1) @ w_out[l]
        h2 = rmsnorm(x, norm_ffn[l])
        if l % 2 == 0:
            j = l // 2
            x = x + swiglu(h2, ffn_gate[j], ffn_up[j], ffn_down[j])
        else:
            j = l // 2
            x = x + moe_swiglu(h2, router[j], exp_gate[j], exp_up[j], exp_down[j])
    return x


def setup_inputs(seed: int = 0) -> dict:
    key = jax.random.key(seed)
    ks = jax.random.split(key, 20)
    f32 = jnp.float32

    def nrm(k, shape, scale):
        return jax.random.normal(k, shape, f32) * scale

    return {
        "x_prompt": nrm(ks[0], (BATCH, SEQ, D_MODEL), 1.0),
        "x_sample": nrm(ks[1], (DEC_BATCH, DEC_SEQ, D_MODEL), 1.0),
        "norm_mix": 1.0 + nrm(ks[2], (DEPTH, D_MODEL), 0.02),
        "w_in": nrm(ks[3], (DEPTH, D_MODEL, IN_COLS), D_MODEL ** -0.5),
        "lb_logits": nrm(ks[4], (2, DEPTH, HG_KEY), 0.5),
        "hg_gnorm": 1.0 + nrm(ks[5], (DEPTH, HG_DV), 0.02),
        "na_qnorm": 1.0 + nrm(ks[6], (DEPTH, NA_DH), 0.02),
        "na_knorm": 1.0 + nrm(ks[7], (DEPTH, NA_DH), 0.02),
        "na_rpb": nrm(ks[8], (DEPTH, NA_HEADS, 2 * WIN_H - 1, 2 * WIN_W - 1), 0.1),
        "w_out": nrm(ks[9], (DEPTH, D_MIX, D_MODEL), D_MIX ** -0.5),
        "norm_ffn": 1.0 + nrm(ks[10], (DEPTH, D_MODEL), 0.02),
        "ffn_gate": nrm(ks[11], (N_DENSE, D_MODEL, FF_DENSE), D_MODEL ** -0.5),
        "ffn_up": nrm(ks[12], (N_DENSE, D_MODEL, FF_DENSE), D_MODEL ** -0.5),
        "ffn_down": nrm(ks[13], (N_DENSE, FF_DENSE, D_MODEL), FF_DENSE ** -0.5),
        "router": nrm(ks[14], (N_MOE, D_MODEL, N_EXPERTS), D_MODEL ** -0.5),
        "exp_gate": nrm(ks[15], (N_MOE, N_EXPERTS, D_MODEL, FF_EXPERT), D_MODEL ** -0.5),
        "exp_up": nrm(ks[16], (N_MOE, N_EXPERTS, D_MODEL, FF_EXPERT), D_MODEL ** -0.5),
        "exp_down": nrm(ks[17], (N_MOE, N_EXPERTS, FF_EXPERT, D_MODEL), FF_EXPERT ** -0.5),
    }


def reference(x_prompt, x_sample, norm_mix, w_in, lb_logits, hg_gnorm, na_qnorm, na_knorm, na_rpb,
              w_out, norm_ffn, ffn_gate, ffn_up, ffn_down, router, exp_gate, exp_up, exp_down):
    lb = jnp.cumsum(jax.nn.softmax(lb_logits.astype(jnp.float32), axis=1), axis=1)
    lb_all = lb - lb[:, :1]
    y_prompt = trunk(x_prompt, norm_mix, w_in, lb_all, hg_gnorm, na_qnorm, na_knorm, na_rpb, w_out,
                     norm_ffn, ffn_gate, ffn_up, ffn_down, router, exp_gate, exp_up, exp_down)
    y_sample = trunk(x_sample, norm_mix, w_in, lb_all, hg_gnorm, na_qnorm, na_knorm, na_rpb, w_out,
                     norm_ffn, ffn_gate, ffn_up, ffn_down, router, exp_gate, exp_up, exp_down)
    return (y_prompt, y_sample)
```

```python
import functools

import numpy as np
import jax
import jax.numpy as jnp
from jax import lax
from jax.experimental import pallas as pl
from jax.experimental.pallas import tpu as pltpu

F32 = jnp.float32
BF16 = jnp.bfloat16

GRID_W = 64
HG_HEADS, HG_D = 4, 128
HG_W = HG_HEADS * HG_D
NA_HEADS, NA_DH = 8, 64
NA_W = NA_HEADS * NA_DH
WIN_H, WIN_W = 8, 16
N_EXPERTS, TOP_K = 8, 2
EPS = 1e-6
NEG = -1e30

LANES = 128
VMEM_LIMIT = 48 * 1024 * 1024
TM = 512
HG_TB = 512
HG_C = 64
NA_RB = 8
FFN_TM = 1024
GATHER_ROWS = 1024


def _dot(a, b):
    return jnp.dot(a, b, preferred_element_type=F32)


def _dot_nt(a, b):
    return lax.dot_general(a, b, (((1,), (1,)), ((), ())), preferred_element_type=F32)


def _dot_tn(a, b):
    return lax.dot_general(a, b, (((0,), (0,)), ((), ())), preferred_element_type=F32)


def _params(*sem):
    return pltpu.CompilerParams(dimension_semantics=sem, vmem_limit_bytes=VMEM_LIMIT)


def _resident(shape):
    nd = len(shape)
    return pl.BlockSpec(shape, lambda *_: (0,) * nd, pipeline_mode=pl.Buffered(1))


def _inproj_body(x_ref, nw_ref, w_ref, qkw_ref, seg_ref, uhg_ref, una_ref):
    x = x_ref[...]
    ms = jnp.mean(x * x, axis=-1, keepdims=True)
    h = (x * lax.rsqrt(ms + EPS) * nw_ref[...]).astype(BF16)
    n_hg = uhg_ref.shape[1]
    for c in range(0, n_hg, 512):
        uhg_ref[:, c:c + 512] = _dot(h, w_ref[:, c:c + 512])
    seg = seg_ref[...]
    for part in range(3):
        for c in range(0, NA_W, LANES):
            col = part * NA_W + c
            y = _dot(h, w_ref[:, n_hg + col:n_hg + col + LANES])
            if part < 2:
                msq = _dot((y * y).astype(BF16), seg)
                y = y * lax.rsqrt(msq + EPS) * qkw_ref[part:part + 1, c:c + LANES]
            una_ref[:, col:col + LANES] = y.astype(BF16)


def _inproj(x, nw, w, qkw, seg):
    n, d = x.shape
    n_hg = w.shape[1] - 3 * NA_W
    return pl.pallas_call(
        _inproj_body,
        grid=(n // TM,),
        in_specs=[
            pl.BlockSpec((TM, d), lambda i: (i, 0)),
            _resident(nw.shape), _resident(w.shape), _resident(qkw.shape), _resident(seg.shape),
        ],
        out_specs=[
            pl.BlockSpec((TM, n_hg), lambda i: (i, 0)),
            pl.BlockSpec((TM, 3 * NA_W), lambda i: (i, 0)),
        ],
        out_shape=[
            jax.ShapeDtypeStruct((n, n_hg), F32),
            jax.ShapeDtypeStruct((n, 3 * NA_W), BF16),
        ],
        compiler_params=_params("arbitrary"),
        name="inproj",
    )(x, nw, w, qkw, seg)


def _split3(x):
    hi = x.astype(BF16)
    r = x - hi.astype(F32)
    mid = r.astype(BF16)
    lo = (r - mid.astype(F32)).astype(BF16)
    return hi, mid, lo


def _hgrn_chunk(q_ref, f_ref, v_ref, lb_ref, tri_ref, o_ref, st_ref, r0, reverse):
    c = HG_C
    rows = pl.ds(r0, c)
    q = q_ref[rows, :]
    x = f_ref[rows, :]
    v = v_ref[rows, :]
    la = lb_ref[0:1, :]
    l1 = lb_ref[1:2, :]
    q = q / (1.0 + jnp.exp(-q))
    ls = jnp.minimum(x, 0.0) - jnp.log1p(jnp.exp(-jnp.abs(x)))
    cc = l1 + ls
    g = jnp.maximum(la, cc) + jnp.log1p(jnp.exp(-jnp.abs(la - cc)))
    k = jnp.exp(cc - x)

    bc = _dot(tri_ref[...], jnp.concatenate(_split3(g), axis=1))
    b = bc[:, :HG_W] + bc[:, HG_W:2 * HG_W] + bc[:, 2 * HG_W:]

    row = lax.broadcasted_iota(jnp.int32, (c, HG_D), 0)
    ri = lax.broadcasted_iota(jnp.int32, (c, c), 0)
    ci = lax.broadcasted_iota(jnp.int32, (c, c), 1)
    nb = c // 8
    i8 = lax.broadcasted_iota(jnp.int32, (nb, 8, HG_D), 1)

    for h in range(HG_HEADS):
        sl = slice(h * HG_D, (h + 1) * HG_D)
        bh, qh, kh, vh = b[:, sl], q[:, sl], k[:, sl], v[:, sl]
        vb = vh.astype(BF16)
        st = st_ref[h]
        btot = bh[0:1, :] if reverse else bh[c - 1:c, :]
        qe = (qh * jnp.exp(bh)).astype(BF16)
        ke = (kh * jnp.exp(btot - bh)).astype(BF16)
        o = _dot_nt(qe, st.astype(BF16))

        a = jnp.zeros((c, c), F32)
        w = c // 2
        while w >= 8:
            grp = c // (2 * w)
            b4 = bh.reshape(grp, 2 * w, HG_D)
            edge = w if reverse else w - 1
            r = jnp.broadcast_to(b4[:, edge:edge + 1, :], (grp, 2 * w, HG_D)).reshape(c, HG_D)
            e = jnp.exp(-jnp.abs(bh - r))
            late = (row & (2 * w - 1)) >= w
            is_q = jnp.logical_not(late) if reverse else late
            qs = jnp.where(is_q, qh * e, 0.0).astype(BF16)
            ks = jnp.where(is_q, 0.0, kh * e).astype(BF16)
            aw = _dot_nt(qs, ks)
            if grp > 1:
                sh = int(np.log2(2 * w))
                aw = jnp.where((ri >> sh) == (ci >> sh), aw, 0.0)
            a = a + aw
            w //= 2
        o = o + _dot(a.astype(BF16), vb)

        b3, q3, k3, v3 = (t.reshape(nb, 8, HG_D) for t in (bh, qh, kh, vh))
        od = jnp.zeros((nb, 8, HG_D), F32)
        for j in range(8):
            valid = (i8 <= j) if reverse else (i8 >= j)
            e = jnp.where(valid, jnp.exp(jnp.minimum(b3 - b3[:, j:j + 1, :], 0.0)), 0.0)
            aj = jnp.sum(q3 * e * k3[:, j:j + 1, :], axis=-1, keepdims=True)
            od = od + aj * v3[:, j:j + 1, :]
        o_ref[rows, sl] = o + od.reshape(c, HG_D)

        st_ref[h] = st * jnp.exp(btot) + _dot_tn(vb, ke)


def _hgrn_body(fwd_ref, bwd_ref, first_ref, qf_ref, ff_ref, vf_ref, qb_ref, fb_ref, vb_ref,
               lbf_ref, lbb_ref, tril_ref, triu_ref, of_ref, ob_ref, stf_ref, stb_ref):
    del fwd_ref, bwd_ref
    g = pl.program_id(0)

    @pl.when(first_ref[g] == 1)
    def _():
        stf_ref[...] = jnp.zeros_like(stf_ref)
        stb_ref[...] = jnp.zeros_like(stb_ref)

    n_chunks = HG_TB // HG_C

    def body(ci, carry):
        rf = pl.multiple_of(ci * HG_C, HG_C)
        _hgrn_chunk(qf_ref, ff_ref, vf_ref, lbf_ref, tril_ref, of_ref, stf_ref, rf, False)
        rb = pl.multiple_of((n_chunks - 1 - ci) * HG_C, HG_C)
        _hgrn_chunk(qb_ref, fb_ref, vb_ref, lbb_ref, triu_ref, ob_ref, stb_ref, rb, True)
        return carry

    lax.fori_loop(0, n_chunks, body, 0)


def _hgrn(uhg, lbf, lbb, seq_lens):
    n = uhg.shape[0]
    fwd, bwd, first = [], [], []
    base = 0
    for t in seq_lens:
        nt = t // HG_TB
        for j in range(nt):
            fwd.append(base + j)
            bwd.append(base + nt - 1 - j)
            first.append(1 if j == 0 else 0)
        base += nt
    fwd, bwd, first = (jnp.asarray(np.array(a, np.int32)) for a in (fwd, bwd, first))
    tril = jnp.asarray(np.tril(np.ones((HG_C, HG_C), np.float32)), BF16)
    triu = jnp.asarray(np.triu(np.ones((HG_C, HG_C), np.float32)), BF16)

    def spec(which, col):
        if which == 0:
            return pl.BlockSpec((HG_TB, HG_W), lambda g, f, b, s: (f[g], col))
        return pl.BlockSpec((HG_TB, HG_W), lambda g, f, b, s: (b[g], col))

    const = lambda shape: pl.BlockSpec(shape, lambda g, f, b, s: (0, 0))
    grid_spec = pltpu.PrefetchScalarGridSpec(
        num_scalar_prefetch=3,
        grid=(n // HG_TB,),
        in_specs=[spec(0, 0), spec(0, 1), spec(0, 3), spec(1, 0), spec(1, 2), spec(1, 3),
                  const(lbf.shape), const(lbb.shape), const(tril.shape), const(triu.shape)],
        out_specs=[spec(0, 0), spec(1, 0)],
        scratch_shapes=[pltpu.VMEM((HG_HEADS, HG_D, HG_D), F32),
                        pltpu.VMEM((HG_HEADS, HG_D, HG_D), F32)],
    )
    return pl.pallas_call(
        _hgrn_body,
        grid_spec=grid_spec,
        out_shape=[jax.ShapeDtypeStruct((n, HG_W), F32), jax.ShapeDtypeStruct((n, HG_W), F32)],
        compiler_params=_params("arbitrary"),
        name="hgrn",
    )(fwd, bwd, first, uhg, uhg, uhg, uhg, uhg, uhg, lbf, lbb, tril, triu)


def _na_body(prev_ref, next_ref, rloc_ref, rtot_ref, q_ref, kp_ref, kc_ref, kn_ref,
             vp_ref, vc_ref, vn_ref, bias_ref, o_ref, ks_ref, vs_ref):
    del prev_ref, next_ref
    g = pl.program_id(1)
    blk = NA_RB * GRID_W
    ks_ref[0:blk, :] = kp_ref[...]
    ks_ref[blk:2 * blk, :] = kc_ref[...]
    ks_ref[2 * blk:3 * blk, :] = kn_ref[...]
    vs_ref[0:blk, :] = vp_ref[...]
    vs_ref[blk:2 * blk, :] = vc_ref[...]
    vs_ref[2 * blk:3 * blk, :] = vn_ref[...]
    r0 = rloc_ref[g]
    n_rows = rtot_ref[g]
    first_head = lax.broadcasted_iota(jnp.int32, (GRID_W, LANES), 1) < NA_DH
    nkeys = WIN_H * GRID_W

    def row_body(rl, carry):
        r = r0 + rl
        rs = jnp.clip(r - WIN_H // 2, 0, n_rows - WIN_H)
        off = pl.multiple_of((rs - r0 + NA_RB) * GRID_W, GRID_W)
        q2 = q_ref[pl.ds(pl.multiple_of(rl * GRID_W, GRID_W), GRID_W), :]
        zero = jnp.zeros_like(q2)
        qq = jnp.concatenate([jnp.where(first_head, q2, zero), jnp.where(first_head, zero, q2)], axis=0)
        kw = ks_ref[pl.ds(off, nkeys), :]
        vw = vs_ref[pl.ds(off, nkeys), :]
        s = _dot_nt(qq, kw) + bias_ref[0, rs - r + WIN_H - 1]
        p = jnp.exp(s - jnp.max(s, axis=-1, keepdims=True))
        l = jnp.sum(p, axis=-1, keepdims=True)
        ov = _dot(p.astype(BF16), vw) / l
        out = jnp.where(first_head, ov[:GRID_W], ov[GRID_W:])
        o_ref[pl.ds(pl.multiple_of(rl * GRID_W, GRID_W), GRID_W), :] = out.astype(o_ref.dtype)
        return carry

    lax.fori_loop(0, NA_RB, row_body, 0)


def _natten(una, bias, seq_lens):
    n = una.shape[0]
    blk = NA_RB * GRID_W
    prev, nxt, rloc, rtot = [], [], [], []
    base = 0
    for t in seq_lens:
        rows = t // GRID_W
        nb = rows // NA_RB
        for j in range(nb):
            prev.append(base + max(j - 1, 0))
            nxt.append(base + min(j + 1, nb - 1))
            rloc.append(j * NA_RB)
            rtot.append(rows)
        base += nb
    prev, nxt, rloc, rtot = (jnp.asarray(np.array(a, np.int32)) for a in (prev, nxt, rloc, rtot))
    npair = NA_W // LANES

    def cur(part):
        return pl.BlockSpec((blk, LANES), lambda p, g, pv, nx, rl, rt: (g, part * npair + p))

    def nbr(part, which):
        if which == 0:
            return pl.BlockSpec((blk, LANES), lambda p, g, pv, nx, rl, rt: (pv[g], part * npair + p))
        return pl.BlockSpec((blk, LANES), lambda p, g, pv, nx, rl, rt: (nx[g], part * npair + p))

    grid_spec = pltpu.PrefetchScalarGridSpec(
        num_scalar_prefetch=4,
        grid=(npair, n // blk),
        in_specs=[cur(0), nbr(1, 0), cur(1), nbr(1, 1), nbr(2, 0), cur(2), nbr(2, 1),
                  pl.BlockSpec((1,) + bias.shape[1:], lambda p, g, pv, nx, rl, rt: (p, 0, 0, 0))],
        out_specs=pl.BlockSpec((blk, LANES), lambda p, g, pv, nx, rl, rt: (g, p)),
        scratch_shapes=[pltpu.VMEM((3 * blk, LANES), BF16), pltpu.VMEM((3 * blk, LANES), BF16)],
    )
    return pl.pallas_call(
        _na_body,
        grid_spec=grid_spec,
        out_shape=jax.ShapeDtypeStruct((n, NA_W), BF16),
        compiler_params=_params("arbitrary", "arbitrary"),
        name="natten",
    )(prev, nxt, rloc, rtot, una, una, una, una, una, una, una, bias)


def _na_bias_table(rpb):
    c = np.arange(GRID_W)
    cstart = np.clip(c - WIN_W // 2, 0, GRID_W - WIN_W)
    kc = np.arange(GRID_W)
    mask = (kc[None, :] >= cstart[:, None]) & (kc[None, :] < cstart[:, None] + WIN_W)
    dx = np.clip(kc[None, :] - c[:, None] + WIN_W - 1, 0, 2 * WIN_W - 2)
    dy = np.arange(WIN_H)[:, None] + np.arange(WIN_H)[None, :]
    t = rpb.astype(F32)[:, dy[:, :, None, None], dx[None, None, :, :]]
    t = jnp.where(jnp.asarray(mask)[None, None, None], t, NEG)
    t = t.transpose(0, 1, 3, 2, 4).reshape(NA_HEADS, WIN_H, GRID_W, WIN_H * GRID_W)
    t = t.reshape(NA_HEADS // 2, 2, WIN_H, GRID_W, WIN_H * GRID_W).transpose(0, 2, 1, 3, 4)
    return t.reshape(NA_HEADS // 2, WIN_H, 2 * GRID_W, WIN_H * GRID_W)


def _outproj_body(of_ref, ob_ref, g_ref, ona_ref, x_ref, gw_ref, w_ref, nw_ref, xo_ref, h2_ref):
    o = of_ref[...] + ob_ref[...]
    g = g_ref[...]
    gate = g / (1.0 + jnp.exp(-g))
    parts = []
    for h in range(HG_HEADS):
        sl = slice(h * HG_D, (h + 1) * HG_D)
        oh = o[:, sl]
        ms = jnp.mean(oh * oh, axis=-1, keepdims=True)
        parts.append(oh * lax.rsqrt(ms + EPS) * gw_ref[...] * gate[:, sl])
    hg = jnp.concatenate(parts, axis=1).astype(BF16)
    y = _dot(hg, w_ref[0:HG_W, :]) + _dot(ona_ref[...], w_ref[HG_W:, :])
    x = x_ref[...] + y
    xo_ref[...] = x
    ms = jnp.mean(x * x, axis=-1, keepdims=True)
    h2_ref[...] = (x * lax.rsqrt(ms + EPS) * nw_ref[...]).astype(BF16)


def _outproj(o_f, o_b, uhg, o_na, x, gw, w, nw):
    n, d = x.shape
    tok = lambda width, col=0: pl.BlockSpec((TM, width), lambda i: (i, col))
    return pl.pallas_call(
        _outproj_body,
        grid=(n // TM,),
        in_specs=[tok(HG_W), tok(HG_W), tok(HG_W, 4), tok(NA_W), tok(d),
                  _resident(gw.shape), _resident(w.shape), _resident(nw.shape)],
        out_specs=[tok(d), tok(d)],
        out_shape=[jax.ShapeDtypeStruct((n, d), F32), jax.ShapeDtypeStruct((n, d), BF16)],
        compiler_params=_params("arbitrary"),
        name="outproj",
    )(o_f, o_b, uhg, o_na, x, gw, w, nw)


def _ffn_body(te_ref, nv_ref, x_ref, wg_ref, wu_ref, wd_ref, *rest, residual):
    del te_ref
    if residual:
        res_ref, o_ref, acc_ref = rest
    else:
        o_ref, acc_ref = rest
    i = pl.program_id(0)
    f = pl.program_id(1)

    @pl.when(i < nv_ref[0])
    def _():
        x = x_ref[...]
        gate = _dot(x, wg_ref[0])
        up = _dot(x, wu_ref[0])
        hid = (gate / (1.0 + jnp.exp(-gate)) * up).astype(BF16)
        part = _dot(hid, wd_ref[0])

        @pl.when(f == 0)
        def _():
            acc_ref[...] = part

        @pl.when(f != 0)
        def _():
            acc_ref[...] += part

    last = f == pl.num_programs(1) - 1

    @pl.when(jnp.logical_and(last, i < nv_ref[0]))
    def _():
        if residual:
            o_ref[...] = res_ref[...] + acc_ref[...]
        else:
            o_ref[...] = acc_ref[...]

    @pl.when(jnp.logical_and(last, i >= nv_ref[0]))
    def _():
        o_ref[...] = jnp.zeros_like(o_ref)


def _ffn(x, wg, wu, wd, tile_expert, n_valid, tm, tf, res=None):
    p, d = x.shape
    ff = wg.shape[2]
    residual = res is not None
    row = pl.BlockSpec((tm, d), lambda i, f, te, nv: (i, 0))
    in_specs = [row,
                pl.BlockSpec((1, d, tf), lambda i, f, te, nv: (te[i], 0, f)),
                pl.BlockSpec((1, d, tf), lambda i, f, te, nv: (te[i], 0, f)),
                pl.BlockSpec((1, tf, d), lambda i, f, te, nv: (te[i], f, 0))]
    args = [x, wg, wu, wd]
    if residual:
        in_specs.append(row)
        args.append(res)
    grid_spec = pltpu.PrefetchScalarGridSpec(
        num_scalar_prefetch=2,
        grid=(p // tm, ff // tf),
        in_specs=in_specs,
        out_specs=row,
        scratch_shapes=[pltpu.VMEM((tm, d), F32)],
    )
    return pl.pallas_call(
        functools.partial(_ffn_body, residual=residual),
        grid_spec=grid_spec,
        out_shape=jax.ShapeDtypeStruct((p, d), F32),
        compiler_params=_params("arbitrary", "arbitrary"),
        name="ffn_res" if residual else "ffn_grouped",
    )(tile_expert, n_valid, *args)


def _router_body(x_ref, nw_ref, whi_ref, wlo_ref, ids_ref, wts_ref):
    x = x_ref[...]
    ms = jnp.mean(x * x, axis=-1, keepdims=True)
    h = x * lax.rsqrt(ms + EPS) * nw_ref[...]
    hh = h.astype(BF16)
    hl = (h - hh.astype(F32)).astype(BF16)
    logits = _dot(hh, whi_ref[...]) + (_dot(hl, whi_ref[...]) + _dot(hh, wlo_ref[...]))
    col = lax.broadcasted_iota(jnp.int32, logits.shape, 1).astype(F32)
    logits = jnp.where(col < N_EXPERTS, logits, -jnp.inf)
    m1 = jnp.max(logits, axis=-1, keepdims=True)
    i1 = jnp.min(jnp.where(logits == m1, col, float(LANES)), axis=-1, keepdims=True)
    rest = jnp.where(col == i1, -jnp.inf, logits)
    m2 = jnp.max(rest, axis=-1, keepdims=True)
    i2 = jnp.min(jnp.where(rest == m2, col, float(LANES)), axis=-1, keepdims=True)
    e = jnp.exp(m2 - m1)
    w1 = 1.0 / (1.0 + e)
    w2 = e / (1.0 + e)
    ids_ref[...] = jnp.where(col == 0.0, i1, jnp.where(col == 1.0, i2, 0.0)).astype(jnp.int32)
    wts_ref[...] = jnp.where(col == 0.0, w1, jnp.where(col == 1.0, w2, 0.0))


def _router(x, nw, whi, wlo):
    n, d = x.shape
    return pl.pallas_call(
        _router_body,
        grid=(n // TM,),
        in_specs=[pl.BlockSpec((TM, d), lambda i: (i, 0)),
                  _resident(nw.shape), _resident(whi.shape), _resident(wlo.shape)],
        out_specs=[pl.BlockSpec((TM, LANES), lambda i: (i, 0)), pl.BlockSpec((TM, LANES), lambda i: (i, 0))],
        out_shape=[jax.ShapeDtypeStruct((n, LANES), jnp.int32), jax.ShapeDtypeStruct((n, LANES), F32)],
        compiler_params=_params("arbitrary"),
        name="router",
    )(x, nw, whi, wlo)


def _row_copy(src_ref, dst_ref, sem, src_row, dst_row):
    return pltpu.make_async_copy(src_ref.at[pl.ds(src_row, 1)], dst_ref.at[pl.ds(dst_row, 1)], sem)


def _gather_body(idx_hbm, src_hbm, o_ref, idx_smem, sem_idx, sem_rows):
    i = pl.program_id(0)
    cp = pltpu.make_async_copy(idx_hbm.at[pl.ds(i * GATHER_ROWS, GATHER_ROWS)], idx_smem, sem_idx)
    cp.start()
    cp.wait()

    def start(r, carry):
        _row_copy(src_hbm, o_ref, sem_rows, idx_smem[r], r).start()
        return carry

    lax.fori_loop(0, GATHER_ROWS, start, 0)

    def wait(r, carry):
        _row_copy(src_hbm, o_ref, sem_rows, 0, r).wait()
        return carry

    lax.fori_loop(0, GATHER_ROWS, wait, 0)


def _gather_rows(idx, src):
    p = idx.shape[0]
    w = src.shape[1]
    return pl.pallas_call(
        _gather_body,
        grid=(p // GATHER_ROWS,),
        in_specs=[pl.BlockSpec(memory_space=pl.ANY), pl.BlockSpec(memory_space=pl.ANY)],
        out_specs=pl.BlockSpec((GATHER_ROWS, w), lambda i: (i, 0)),
        out_shape=jax.ShapeDtypeStruct((p, w), src.dtype),
        scratch_shapes=[pltpu.SMEM((GATHER_ROWS,), jnp.int32),
                        pltpu.SemaphoreType.DMA(()), pltpu.SemaphoreType.DMA(())],
        compiler_params=_params("arbitrary"),
        name="gather_rows",
    )(idx, src)


def _combine_body(pos_hbm, y_hbm, x_ref, wts_ref, o_ref, pos_smem, buf_ref, sem_idx, sem_rows):
    i = pl.program_id(0)
    tc = GATHER_ROWS // TOP_K
    cp = pltpu.make_async_copy(pos_hbm.at[pl.ds(i * GATHER_ROWS, GATHER_ROWS)], pos_smem, sem_idx)
    cp.start()
    cp.wait()

    def start(r, carry):
        _row_copy(y_hbm, buf_ref, sem_rows, pos_smem[r], r).start()
        return carry

    lax.fori_loop(0, GATHER_ROWS, start, 0)

    def wait(r, carry):
        _row_copy(y_hbm, buf_ref, sem_rows, 0, r).wait()
        return carry

    lax.fori_loop(0, GATHER_ROWS, wait, 0)
    w = wts_ref[...]
    o_ref[...] = x_ref[...] + (w[:, 0:1] * buf_ref[0:tc, :] + w[:, 1:2] * buf_ref[tc:2 * tc, :])


def _combine(pos, y, x, wts):
    n, d = x.shape
    tc = GATHER_ROWS // TOP_K
    return pl.pallas_call(
        _combine_body,
        grid=(n // tc,),
        in_specs=[pl.BlockSpec(memory_space=pl.ANY), pl.BlockSpec(memory_space=pl.ANY),
                  pl.BlockSpec((tc, d), lambda i: (i, 0)), pl.BlockSpec((tc, LANES), lambda i: (i, 0))],
        out_specs=pl.BlockSpec((tc, d), lambda i: (i, 0)),
        out_shape=jax.ShapeDtypeStruct((n, d), F32),
        scratch_shapes=[pltpu.SMEM((GATHER_ROWS,), jnp.int32), pltpu.VMEM((GATHER_ROWS, d), F32),
                        pltpu.SemaphoreType.DMA(()), pltpu.SemaphoreType.DMA(())],
        compiler_params=_params("arbitrary"),
        name="combine",
    )(pos, y, x, wts)


def _moe(x, h2, nw, whi, wlo, wg, wu, wd):
    n, d = x.shape
    ids, wts = _router(x, nw, whi, wlo)
    e_flat = ids[:, :TOP_K].reshape(-1)
    na = n * TOP_K
    order = jnp.argsort(e_flat, stable=True).astype(jnp.int32)
    counts = jnp.bincount(e_flat, length=N_EXPERTS).astype(jnp.int32)
    padded = (counts + FFN_TM - 1) // FFN_TM * FFN_TM
    pad_off = jnp.cumsum(padded) - padded
    raw_off = jnp.cumsum(counts) - counts
    e_sorted = e_flat[order]
    pos_sorted = pad_off[e_sorted] + (jnp.arange(na, dtype=jnp.int32) - raw_off[e_sorted])
    n_rows = (na + N_EXPERTS * (FFN_TM - 1)) // FFN_TM * FFN_TM
    src_tok = jnp.zeros((n_rows,), jnp.int32).at[pos_sorted].set(order // TOP_K)
    pos = jnp.zeros((na,), jnp.int32).at[order].set(pos_sorted)
    tile_start = jnp.arange(n_rows // FFN_TM, dtype=jnp.int32) * FFN_TM
    tile_expert = jnp.minimum(
        jnp.sum(tile_start[:, None] >= jnp.cumsum(padded)[None, :], axis=1), N_EXPERTS - 1).astype(jnp.int32)
    n_valid = (jnp.sum(padded) // FFN_TM).astype(jnp.int32).reshape(1)

    h2w = lax.bitcast_convert_type(h2.reshape(n, d // 2, 2), jnp.uint32)
    xs = _gather_rows(src_tok, h2w)
    xs = lax.bitcast_convert_type(xs, BF16).reshape(n_rows, d)
    y = _ffn(xs, wg, wu, wd, tile_expert, n_valid, tm=FFN_TM, tf=512)
    tc = GATHER_ROWS // TOP_K
    pos_tiles = pos.reshape(n // tc, tc, TOP_K).transpose(0, 2, 1).reshape(-1)
    return _combine(pos_tiles, y, x, wts)


def kernel(x_prompt, x_sample, norm_mix, w_in, lb_logits, hg_gnorm, na_qnorm, na_knorm, na_rpb, w_out, norm_ffn,
           ffn_gate, ffn_up, ffn_down, router, exp_gate, exp_up, exp_down):
    bp, tp, d = x_prompt.shape
    bs, ts, _ = x_sample.shape
    depth = w_in.shape[0]
    seq_lens = [tp] * bp + [ts] * bs
    assert tp % HG_TB == 0 and ts % HG_TB == 0 and min(tp, ts) // GRID_W >= WIN_H
    n = bp * tp + bs * ts
    assert n % FFN_TM == 0 and d % 2 == 0
    x = jnp.concatenate([x_prompt.reshape(bp * tp, d), x_sample.reshape(bs * ts, d)], axis=0)

    lb = jnp.cumsum(jax.nn.softmax(lb_logits.astype(F32), axis=1), axis=1)
    lb = lb - lb[:, :1]
    lbp = jnp.stack([jnp.log(lb), jnp.log1p(-lb)], axis=2)

    seg = jnp.asarray(np.kron(np.eye(LANES // NA_DH), np.ones((NA_DH, NA_DH))) / NA_DH, BF16)
    scale = NA_DH ** -0.5

    for l in range(depth):
        qkw = jnp.stack([jnp.tile(na_qnorm[l].astype(F32), NA_HEADS) * scale,
                         jnp.tile(na_knorm[l].astype(F32), NA_HEADS)])
        uhg, una = _inproj(x, norm_mix[l].reshape(1, d), w_in[l].astype(BF16), qkw, seg)
        o_f, o_b = _hgrn(uhg, lbp[0, l], lbp[1, l], seq_lens)
        o_na = _natten(una, _na_bias_table(na_rpb[l]), seq_lens)
        x, h2 = _outproj(o_f, o_b, uhg, o_na, x, hg_gnorm[l].reshape(1, HG_D), w_out[l].astype(BF16),
                         norm_ffn[l].reshape(1, d))
        j = l // 2
        if l % 2 == 0:
            n_tiles = n // TM
            x = _ffn(h2, ffn_gate[j:j + 1].astype(BF16), ffn_up[j:j + 1].astype(BF16),
                     ffn_down[j:j + 1].astype(BF16), jnp.zeros((n_tiles,), jnp.int32),
                     jnp.full((1,), n_tiles, jnp.int32), tm=TM, tf=ffn_gate.shape[2] // 2, res=x)
        else:
            wr = jnp.zeros((d, LANES), F32).at[:, :N_EXPERTS].set(router[j].astype(F32))
            whi = wr.astype(BF16)
            wlo = (wr - whi.astype(F32)).astype(BF16)
            x = _moe(x, h2, norm_ffn[l].reshape(1, d), whi, wlo, exp_gate[j].astype(BF16),
                     exp_up[j].astype(BF16), exp_down[j].astype(BF16))

    y_prompt = x[:bp * tp].reshape(bp, tp, d)
    y_sample = x[bp * tp:].reshape(bs, ts, d)
    return (y_prompt, y_sample)
```

```python
import functools

import numpy as np
import jax
import jax.numpy as jnp
from jax import lax
from jax.experimental import pallas as pl
from jax.experimental.pallas import tpu as pltpu

F32 = jnp.float32
BF16 = jnp.bfloat16

GRID_W = 64
HG_HEADS, HG_D = 4, 128
HG_W = HG_HEADS * HG_D
NA_HEADS, NA_DH = 8, 64
NA_W = NA_HEADS * NA_DH
WIN_H, WIN_W = 8, 16
N_EXPERTS, TOP_K = 8, 2
EPS = 1e-6
NEG = -1e30

LANES = 128
MXU_N = 256
DMA_UNROLL = 8
VMEM_LIMIT = 48 * 1024 * 1024
TM = 512
HG_TB = 512
HG_C = 64
HG_MAX_EXPONENT = 80.0
NA_RB = 8
FFN_TM = 1024
GATHER_ROWS = 1024


def _dot(a, b):
    return jnp.dot(a, b, preferred_element_type=F32)


def _dot_nt(a, b):
    return lax.dot_general(a, b, (((1,), (1,)), ((), ())), preferred_element_type=F32)


def _dot_tn(a, b):
    return lax.dot_general(a, b, (((0,), (0,)), ((), ())), preferred_element_type=F32)


def _params(*sem):
    return pltpu.CompilerParams(dimension_semantics=sem, vmem_limit_bytes=VMEM_LIMIT)


def _resident(shape):
    nd = len(shape)
    return pl.BlockSpec(shape, lambda *_: (0,) * nd, pipeline_mode=pl.Buffered(1))


def _inproj_body(x_ref, nw_ref, w_ref, qkw_ref, seg_ref, uhg_ref, una_ref):
    x = x_ref[...]
    ms = jnp.mean(x * x, axis=-1, keepdims=True)
    h = (x * lax.rsqrt(ms + EPS) * nw_ref[...]).astype(BF16)
    n_hg = uhg_ref.shape[1]
    for c in range(0, n_hg, 512):
        uhg_ref[:, c:c + 512] = _dot(h, w_ref[:, c:c + 512])
    seg = seg_ref[...]
    cw = seg.shape[0]
    for part in range(3):
        for c in range(0, NA_W, cw):
            col = part * NA_W + c
            y = _dot(h, w_ref[:, n_hg + col:n_hg + col + cw])
            if part < 2:
                msq = _dot((y * y).astype(BF16), seg)
                y = y * lax.rsqrt(msq + EPS) * qkw_ref[part:part + 1, c:c + cw]
            una_ref[:, col:col + cw] = y.astype(BF16)


def _inproj(x, nw, w, qkw, seg):
    n, d = x.shape
    n_hg = w.shape[1] - 3 * NA_W
    return pl.pallas_call(
        _inproj_body,
        grid=(n // TM,),
        in_specs=[
            pl.BlockSpec((TM, d), lambda i: (i, 0)),
            _resident(nw.shape), _resident(w.shape), _resident(qkw.shape), _resident(seg.shape),
        ],
        out_specs=[
            pl.BlockSpec((TM, n_hg), lambda i: (i, 0)),
            pl.BlockSpec((TM, 3 * NA_W), lambda i: (i, 0)),
        ],
        out_shape=[
            jax.ShapeDtypeStruct((n, n_hg), F32),
            jax.ShapeDtypeStruct((n, 3 * NA_W), BF16),
        ],
        compiler_params=_params("arbitrary"),
        name="inproj",
    )(x, nw, w, qkw, seg)


def _split3(x):
    hi = x.astype(BF16)
    r = x - hi.astype(F32)
    mid = r.astype(BF16)
    lo = (r - mid.astype(F32)).astype(BF16)
    return hi, mid, lo


def _hgrn_prep(q_ref, f_ref, lb_ref, tri_ref, qs_ref, ks_ref, bs_ref, r0, reverse):
    c = HG_C
    rows = pl.ds(r0, c)
    q = q_ref[rows, :]
    x = f_ref[rows, :]
    la = lb_ref[0:1, :]
    l1 = lb_ref[1:2, :]
    ls = jnp.minimum(x, 0.0) - jnp.log1p(jnp.exp(-jnp.abs(x)))
    cc = l1 + ls
    g = jnp.maximum(la, cc) + jnp.log1p(jnp.exp(-jnp.abs(la - cc)))
    bc = _dot(tri_ref[...], jnp.concatenate(_split3(g), axis=1))
    b = bc[:, :HG_W] + bc[:, HG_W:2 * HG_W] + bc[:, 2 * HG_W:]
    qs_ref[rows, :] = q / (1.0 + jnp.exp(-q))
    ks_ref[rows, :] = jnp.exp(cc - x)
    bs_ref[rows, :] = b
    m = c // 2
    if reverse:
        near, total = b[m:m + 1, :], b[0:1, :]
    else:
        near, total = b[m - 1:m, :], b[c - 1:c, :]
    return jnp.maximum(-near, near - total)


def _hgrn_chunk(q_ref, k_ref, b_ref, v_ref, o_ref, st_ref, r0, reverse, fast):
    c = HG_C
    rows = pl.ds(r0, c)
    q = q_ref[rows, :]
    k = k_ref[rows, :]
    b = b_ref[rows, :]
    v = v_ref[rows, :]

    row = lax.broadcasted_iota(jnp.int32, (c, HG_D), 0)
    ri = lax.broadcasted_iota(jnp.int32, (c, c), 0)
    ci = lax.broadcasted_iota(jnp.int32, (c, c), 1)
    nb = c // 8
    i8 = lax.broadcasted_iota(jnp.int32, (nb, 8, HG_D), 1)

    for h in range(HG_HEADS):
        sl = slice(h * HG_D, (h + 1) * HG_D)
        bh, qh, kh, vh = b[:, sl], q[:, sl], k[:, sl], v[:, sl]
        vb = vh.astype(BF16)
        st = st_ref[h]
        btot = bh[0:1, :] if reverse else bh[c - 1:c, :]
        qe = (qh * jnp.exp(bh)).astype(BF16)
        ke = (kh * jnp.exp(btot - bh)).astype(BF16)
        o = _dot_nt(qe, st.astype(BF16))
        st_ref[h] = st * jnp.exp(btot) + _dot_tn(vb, ke)

        if fast:
            m = c // 2
            r = bh[m:m + 1, :] if reverse else bh[m - 1:m, :]
            qr = (qh * jnp.exp(bh - r)).astype(BF16)
            kr = (kh * jnp.exp(r - bh)).astype(BF16)
            causal = (ri <= ci) if reverse else (ri >= ci)
            a = jnp.where(causal, _dot_nt(qr, kr), 0.0)
            o_ref[rows, sl] = o + _dot(a.astype(BF16), vb)
            continue

        a = jnp.zeros((c, c), F32)
        w = c // 2
        while w >= 8:
            grp = c // (2 * w)
            b4 = bh.reshape(grp, 2 * w, HG_D)
            edge = w if reverse else w - 1
            r = jnp.broadcast_to(b4[:, edge:edge + 1, :], (grp, 2 * w, HG_D)).reshape(c, HG_D)
            e = jnp.exp(-jnp.abs(bh - r))
            late = (row & (2 * w - 1)) >= w
            is_q = jnp.logical_not(late) if reverse else late
            qs = jnp.where(is_q, qh * e, 0.0).astype(BF16)
            ks = jnp.where(is_q, 0.0, kh * e).astype(BF16)
            aw = _dot_nt(qs, ks)
            if grp > 1:
                sh = int(np.log2(2 * w))
                aw = jnp.where((ri >> sh) == (ci >> sh), aw, 0.0)
            a = a + aw
            w //= 2
        o = o + _dot(a.astype(BF16), vb)

        b3, q3, k3, v3 = (t.reshape(nb, 8, HG_D) for t in (bh, qh, kh, vh))
        od = jnp.zeros((nb, 8, HG_D), F32)
        for j in range(8):
            valid = (i8 <= j) if reverse else (i8 >= j)
            e = jnp.exp(jnp.where(valid, b3 - b3[:, j:j + 1, :], NEG))
            aj = jnp.sum(q3 * e * k3[:, j:j + 1, :], axis=-1, keepdims=True)
            od = od + aj * v3[:, j:j + 1, :]
        o_ref[rows, sl] = o + od.reshape(c, HG_D)


def _hgrn_body(fwd_ref, bwd_ref, first_ref, qf_ref, ff_ref, vf_ref, qb_ref, fb_ref, vb_ref,
               lbf_ref, lbb_ref, tril_ref, triu_ref, of_ref, ob_ref,
               stf_ref, stb_ref, qsf_ref, ksf_ref, bsf_ref, qsb_ref, ksb_ref, bsb_ref):
    del fwd_ref, bwd_ref
    g = pl.program_id(0)

    @pl.when(first_ref[g] == 1)
    def _():
        stf_ref[...] = jnp.zeros_like(stf_ref)
        stb_ref[...] = jnp.zeros_like(stb_ref)

    n_chunks = HG_TB // HG_C

    def prep(ci, worst):
        r0 = pl.multiple_of(ci * HG_C, HG_C)
        df = _hgrn_prep(qf_ref, ff_ref, lbf_ref, tril_ref, qsf_ref, ksf_ref, bsf_ref, r0, False)
        db = _hgrn_prep(qb_ref, fb_ref, lbb_ref, triu_ref, qsb_ref, ksb_ref, bsb_ref, r0, True)
        return jnp.maximum(worst, jnp.maximum(df, db))

    worst = jnp.max(lax.fori_loop(0, n_chunks, prep, jnp.zeros((1, HG_W), F32)))
    fits = worst <= HG_MAX_EXPONENT

    def run(fast):
        def body(ci, carry):
            rf = pl.multiple_of(ci * HG_C, HG_C)
            _hgrn_chunk(qsf_ref, ksf_ref, bsf_ref, vf_ref, of_ref, stf_ref, rf, False, fast)
            rb = pl.multiple_of((n_chunks - 1 - ci) * HG_C, HG_C)
            _hgrn_chunk(qsb_ref, ksb_ref, bsb_ref, vb_ref, ob_ref, stb_ref, rb, True, fast)
            return carry

        lax.fori_loop(0, n_chunks, body, 0, unroll=2 if fast else 1)

    @pl.when(fits)
    def _():
        run(True)

    @pl.when(jnp.logical_not(fits))
    def _():
        run(False)


def _hgrn(uhg, lbf, lbb, seq_lens):
    n = uhg.shape[0]
    fwd, bwd, first = [], [], []
    base = 0
    for t in seq_lens:
        nt = t // HG_TB
        for j in range(nt):
            fwd.append(base + j)
            bwd.append(base + nt - 1 - j)
            first.append(1 if j == 0 else 0)
        base += nt
    fwd, bwd, first = (jnp.asarray(np.array(a, np.int32)) for a in (fwd, bwd, first))
    tril = jnp.asarray(np.tril(np.ones((HG_C, HG_C), np.float32)), BF16)
    triu = jnp.asarray(np.triu(np.ones((HG_C, HG_C), np.float32)), BF16)

    def spec(which, col):
        if which == 0:
            return pl.BlockSpec((HG_TB, HG_W), lambda g, f, b, s: (f[g], col))
        return pl.BlockSpec((HG_TB, HG_W), lambda g, f, b, s: (b[g], col))

    const = lambda shape: pl.BlockSpec(shape, lambda g, f, b, s: (0, 0))
    grid_spec = pltpu.PrefetchScalarGridSpec(
        num_scalar_prefetch=3,
        grid=(n // HG_TB,),
        in_specs=[spec(0, 0), spec(0, 1), spec(0, 3), spec(1, 0), spec(1, 2), spec(1, 3),
                  const(lbf.shape), const(lbb.shape), const(tril.shape), const(triu.shape)],
        out_specs=[spec(0, 0), spec(1, 0)],
        scratch_shapes=[pltpu.VMEM((HG_HEADS, HG_D, HG_D), F32)] * 2 + [pltpu.VMEM((HG_TB, HG_W), F32)] * 6,
    )
    return pl.pallas_call(
        _hgrn_body,
        grid_spec=grid_spec,
        out_shape=[jax.ShapeDtypeStruct((n, HG_W), F32), jax.ShapeDtypeStruct((n, HG_W), F32)],
        compiler_params=_params("arbitrary"),
        name="hgrn",
    )(fwd, bwd, first, uhg, uhg, uhg, uhg, uhg, uhg, lbf, lbb, tril, triu)


def _na_body(prev_ref, next_ref, rloc_ref, rtot_ref, q_ref, kp_ref, kc_ref, kn_ref,
             vp_ref, vc_ref, vn_ref, bias_ref, o_ref, ks_ref, vs_ref):
    del prev_ref, next_ref
    g = pl.program_id(1)
    blk = NA_RB * GRID_W
    ks_ref[0:blk, :] = kp_ref[...]
    ks_ref[blk:2 * blk, :] = kc_ref[...]
    ks_ref[2 * blk:3 * blk, :] = kn_ref[...]
    vs_ref[0:blk, :] = vp_ref[...]
    vs_ref[blk:2 * blk, :] = vc_ref[...]
    vs_ref[2 * blk:3 * blk, :] = vn_ref[...]
    r0 = rloc_ref[g]
    n_rows = rtot_ref[g]
    first_head = lax.broadcasted_iota(jnp.int32, (GRID_W, LANES), 1) < NA_DH
    nkeys = WIN_H * GRID_W

    for rl in range(NA_RB):
        r = r0 + rl
        rs = jnp.clip(r - WIN_H // 2, 0, n_rows - WIN_H)
        off = pl.multiple_of((rs - r0 + NA_RB) * GRID_W, GRID_W)
        q2 = q_ref[rl * GRID_W:(rl + 1) * GRID_W, :]
        zero = jnp.zeros_like(q2)
        qq = jnp.concatenate([jnp.where(first_head, q2, zero), jnp.where(first_head, zero, q2)], axis=0)
        kw = ks_ref[pl.ds(off, nkeys), :]
        vw = vs_ref[pl.ds(off, nkeys), :]
        s = _dot_nt(qq, kw) + bias_ref[0, rs - r + WIN_H - 1]
        p = jnp.exp(s - jnp.max(s, axis=-1, keepdims=True))
        l = jnp.sum(p, axis=-1, keepdims=True)
        ov = _dot(p.astype(BF16), vw) / l
        out = jnp.where(first_head, ov[:GRID_W], ov[GRID_W:])
        o_ref[rl * GRID_W:(rl + 1) * GRID_W, :] = out.astype(o_ref.dtype)


def _natten(una, bias, seq_lens):
    n = una.shape[0]
    blk = NA_RB * GRID_W
    prev, nxt, rloc, rtot = [], [], [], []
    base = 0
    for t in seq_lens:
        rows = t // GRID_W
        nb = rows // NA_RB
        for j in range(nb):
            prev.append(base + max(j - 1, 0))
            nxt.append(base + min(j + 1, nb - 1))
            rloc.append(j * NA_RB)
            rtot.append(rows)
        base += nb
    prev, nxt, rloc, rtot = (jnp.asarray(np.array(a, np.int32)) for a in (prev, nxt, rloc, rtot))
    npair = NA_W // LANES

    def cur(part):
        return pl.BlockSpec((blk, LANES), lambda p, g, pv, nx, rl, rt: (g, part * npair + p))

    def nbr(part, which):
        if which == 0:
            return pl.BlockSpec((blk, LANES), lambda p, g, pv, nx, rl, rt: (pv[g], part * npair + p))
        return pl.BlockSpec((blk, LANES), lambda p, g, pv, nx, rl, rt: (nx[g], part * npair + p))

    grid_spec = pltpu.PrefetchScalarGridSpec(
        num_scalar_prefetch=4,
        grid=(npair, n // blk),
        in_specs=[cur(0), nbr(1, 0), cur(1), nbr(1, 1), nbr(2, 0), cur(2), nbr(2, 1),
                  pl.BlockSpec((1,) + bias.shape[1:], lambda p, g, pv, nx, rl, rt: (p, 0, 0, 0))],
        out_specs=pl.BlockSpec((blk, LANES), lambda p, g, pv, nx, rl, rt: (g, p)),
        scratch_shapes=[pltpu.VMEM((3 * blk, LANES), BF16), pltpu.VMEM((3 * blk, LANES), BF16)],
    )
    return pl.pallas_call(
        _na_body,
        grid_spec=grid_spec,
        out_shape=jax.ShapeDtypeStruct((n, NA_W), BF16),
        compiler_params=_params("arbitrary", "arbitrary"),
        name="natten",
    )(prev, nxt, rloc, rtot, una, una, una, una, una, una, una, bias)


def _na_bias_table(rpb):
    c = np.arange(GRID_W)
    cstart = np.clip(c - WIN_W // 2, 0, GRID_W - WIN_W)
    kc = np.arange(GRID_W)
    mask = (kc[None, :] >= cstart[:, None]) & (kc[None, :] < cstart[:, None] + WIN_W)
    dx = np.clip(kc[None, :] - c[:, None] + WIN_W - 1, 0, 2 * WIN_W - 2)
    onehot = np.zeros((2 * WIN_W - 1, GRID_W * GRID_W), np.float32)
    onehot[dx.reshape(-1), np.arange(GRID_W * GRID_W)] = mask.reshape(-1)
    t = jnp.einsum('hyx,xq->hyq', rpb.astype(F32), jnp.asarray(onehot), precision=lax.Precision.HIGHEST)
    t = t + jnp.asarray(np.where(mask.reshape(-1), 0.0, NEG).astype(np.float32))
    t = jnp.stack([t[:, o:o + WIN_H] for o in range(WIN_H)], axis=1)
    t = t.reshape(NA_HEADS, WIN_H, WIN_H, GRID_W, GRID_W)
    t = t.transpose(0, 1, 3, 2, 4).reshape(NA_HEADS, WIN_H, GRID_W, WIN_H * GRID_W)
    t = t.reshape(NA_HEADS // 2, 2, WIN_H, GRID_W, WIN_H * GRID_W).transpose(0, 2, 1, 3, 4)
    return t.reshape(NA_HEADS // 2, WIN_H, 2 * GRID_W, WIN_H * GRID_W)


def _outproj_body(of_ref, ob_ref, g_ref, ona_ref, x_ref, gw_ref, w_ref, xo_ref):
    o = of_ref[...] + ob_ref[...]
    g = g_ref[...]
    gate = g / (1.0 + jnp.exp(-g))
    parts = []
    for h in range(HG_HEADS):
        sl = slice(h * HG_D, (h + 1) * HG_D)
        oh = o[:, sl]
        ms = jnp.mean(oh * oh, axis=-1, keepdims=True)
        parts.append(oh * lax.rsqrt(ms + EPS) * gw_ref[...] * gate[:, sl])
    hg = jnp.concatenate(parts, axis=1).astype(BF16)
    y = _dot(hg, w_ref[0:HG_W, :]) + _dot(ona_ref[...], w_ref[HG_W:, :])
    xo_ref[...] = x_ref[...] + y


def _outproj(o_f, o_b, uhg, o_na, x, gw, w):
    n, d = x.shape
    tok = lambda width, col=0: pl.BlockSpec((TM, width), lambda i: (i, col))
    return pl.pallas_call(
        _outproj_body,
        grid=(n // TM,),
        in_specs=[tok(HG_W), tok(HG_W), tok(HG_W, 4), tok(NA_W), tok(d),
                  _resident(gw.shape), _resident(w.shape)],
        out_specs=tok(d),
        out_shape=jax.ShapeDtypeStruct((n, d), F32),
        compiler_params=_params("arbitrary"),
        name="outproj",
    )(o_f, o_b, uhg, o_na, x, gw, w)


def _ffn_body(te_ref, nv_ref, x_ref, nw_ref, wg_ref, wu_ref, wd_ref, o_ref, xn_ref, acc_ref, *, residual):
    del te_ref
    i = pl.program_id(0)
    f = pl.program_id(1)

    @pl.when(jnp.logical_and(f == 0, i < nv_ref[0]))
    def _():
        xr = x_ref[...]
        ms = jnp.mean(xr * xr, axis=-1, keepdims=True)
        xn_ref[...] = (xr * lax.rsqrt(ms + EPS) * nw_ref[...]).astype(BF16)

    @pl.when(i < nv_ref[0])
    def _():
        x = xn_ref[...]
        gate = _dot(x, wg_ref[0])
        up = _dot(x, wu_ref[0])
        hid = (gate / (1.0 + jnp.exp(-gate)) * up).astype(BF16)
        part = _dot(hid, wd_ref[0])

        @pl.when(f == 0)
        def _():
            acc_ref[...] = part

        @pl.when(f != 0)
        def _():
            acc_ref[...] += part

    last = f == pl.num_programs(1) - 1

    @pl.when(jnp.logical_and(last, i < nv_ref[0]))
    def _():
        if residual:
            o_ref[...] = x_ref[...] + acc_ref[...]
        else:
            o_ref[...] = acc_ref[...]

    @pl.when(jnp.logical_and(last, i >= nv_ref[0]))
    def _():
        o_ref[...] = jnp.zeros_like(o_ref)


def _ffn(x, nw, wg, wu, wd, tile_expert, n_valid, tm, tf, residual):
    p, d = x.shape
    ff = wg.shape[2]
    row = pl.BlockSpec((tm, d), lambda i, f, te, nv: (i, 0))
    grid_spec = pltpu.PrefetchScalarGridSpec(
        num_scalar_prefetch=2,
        grid=(p // tm, ff // tf),
        in_specs=[row,
                  pl.BlockSpec(nw.shape, lambda i, f, te, nv: (0, 0)),
                  pl.BlockSpec((1, d, tf), lambda i, f, te, nv: (te[i], 0, f)),
                  pl.BlockSpec((1, d, tf), lambda i, f, te, nv: (te[i], 0, f)),
                  pl.BlockSpec((1, tf, d), lambda i, f, te, nv: (te[i], f, 0))],
        out_specs=row,
        scratch_shapes=[pltpu.VMEM((tm, d), BF16), pltpu.VMEM((tm, d), F32)],
    )
    return pl.pallas_call(
        functools.partial(_ffn_body, residual=residual),
        grid_spec=grid_spec,
        out_shape=jax.ShapeDtypeStruct((p, d), F32),
        compiler_params=_params("arbitrary", "arbitrary"),
        name="ffn_res" if residual else "ffn_grouped",
    )(tile_expert, n_valid, x, nw, wg, wu, wd)


def _router_body(x_ref, nw_ref, whi_ref, wlo_ref, ids_ref, wts_ref):
    x = x_ref[...]
    ms = jnp.mean(x * x, axis=-1, keepdims=True)
    h = x * lax.rsqrt(ms + EPS) * nw_ref[...]
    hh = h.astype(BF16)
    hl = (h - hh.astype(F32)).astype(BF16)
    logits = _dot(hh, whi_ref[...]) + (_dot(hl, whi_ref[...]) + _dot(hh, wlo_ref[...]))
    col = lax.broadcasted_iota(jnp.int32, logits.shape, 1).astype(F32)
    logits = jnp.where(col < N_EXPERTS, logits, -jnp.inf)
    m1 = jnp.max(logits, axis=-1, keepdims=True)
    i1 = jnp.min(jnp.where(logits == m1, col, float(LANES)), axis=-1, keepdims=True)
    rest = jnp.where(col == i1, -jnp.inf, logits)
    m2 = jnp.max(rest, axis=-1, keepdims=True)
    i2 = jnp.min(jnp.where(rest == m2, col, float(LANES)), axis=-1, keepdims=True)
    e = jnp.exp(m2 - m1)
    w1 = 1.0 / (1.0 + e)
    w2 = e / (1.0 + e)
    ids_ref[...] = jnp.where(col == 0.0, i1, jnp.where(col == 1.0, i2, 0.0)).astype(jnp.int32)
    wts_ref[...] = jnp.where(col == 0.0, w1, jnp.where(col == 1.0, w2, 0.0))


def _router(x, nw, whi, wlo):
    n, d = x.shape
    return pl.pallas_call(
        _router_body,
        grid=(n // TM,),
        in_specs=[pl.BlockSpec((TM, d), lambda i: (i, 0)),
                  _resident(nw.shape), _resident(whi.shape), _resident(wlo.shape)],
        out_specs=[pl.BlockSpec((TM, LANES), lambda i: (i, 0)), pl.BlockSpec((TM, LANES), lambda i: (i, 0))],
        out_shape=[jax.ShapeDtypeStruct((n, LANES), jnp.int32), jax.ShapeDtypeStruct((n, LANES), F32)],
        compiler_params=_params("arbitrary"),
        name="router",
    )(x, nw, whi, wlo)


def _row_copy(src_ref, dst_ref, sem, src_row, dst_row):
    return pltpu.make_async_copy(src_ref.at[pl.ds(src_row, 1)], dst_ref.at[pl.ds(dst_row, 1)], sem)


def _gather_into(idx_smem, src_hbm, dst_ref, sem):
    def start(b, carry):
        for u in range(DMA_UNROLL):
            r = b * DMA_UNROLL + u
            _row_copy(src_hbm, dst_ref, sem, idx_smem[r], r).start(priority=u % 2)
        return carry

    lax.fori_loop(0, GATHER_ROWS // DMA_UNROLL, start, 0)

    def wait(b, carry):
        for u in range(DMA_UNROLL):
            _row_copy(src_hbm, dst_ref, sem, 0, b * DMA_UNROLL + u).wait()
        return carry

    lax.fori_loop(0, GATHER_ROWS // DMA_UNROLL, wait, 0)


def _gather_body(idx_hbm, src_hbm, o_ref, idx_smem, sem_idx, sem_rows):
    i = pl.program_id(0)
    cp = pltpu.make_async_copy(idx_hbm.at[pl.ds(i * GATHER_ROWS, GATHER_ROWS)], idx_smem, sem_idx)
    cp.start()
    cp.wait()
    _gather_into(idx_smem, src_hbm, o_ref, sem_rows)


def _gather_rows(idx, src):
    p = idx.shape[0]
    w = src.shape[1]
    return pl.pallas_call(
        _gather_body,
        grid=(p // GATHER_ROWS,),
        in_specs=[pl.BlockSpec(memory_space=pl.ANY), pl.BlockSpec(memory_space=pl.ANY)],
        out_specs=pl.BlockSpec((GATHER_ROWS, w), lambda i: (i, 0)),
        out_shape=jax.ShapeDtypeStruct((p, w), src.dtype),
        scratch_shapes=[pltpu.SMEM((GATHER_ROWS,), jnp.int32),
                        pltpu.SemaphoreType.DMA(()), pltpu.SemaphoreType.DMA(())],
        compiler_params=_params("arbitrary"),
        name="gather_rows",
    )(idx, src)


def _combine_body(pos_hbm, y_hbm, x_ref, wts_ref, o_ref, pos_smem, buf_ref, sem_idx, sem_rows):
    i = pl.program_id(0)
    tc = GATHER_ROWS // TOP_K
    cp = pltpu.make_async_copy(pos_hbm.at[pl.ds(i * GATHER_ROWS, GATHER_ROWS)], pos_smem, sem_idx)
    cp.start()
    cp.wait()
    _gather_into(pos_smem, y_hbm, buf_ref, sem_rows)
    w = wts_ref[...]
    o_ref[...] = x_ref[...] + (w[:, 0:1] * buf_ref[0:tc, :] + w[:, 1:2] * buf_ref[tc:2 * tc, :])


def _combine(pos, y, x, wts):
    n, d = x.shape
    tc = GATHER_ROWS // TOP_K
    return pl.pallas_call(
        _combine_body,
        grid=(n // tc,),
        in_specs=[pl.BlockSpec(memory_space=pl.ANY), pl.BlockSpec(memory_space=pl.ANY),
                  pl.BlockSpec((tc, d), lambda i: (i, 0)), pl.BlockSpec((tc, LANES), lambda i: (i, 0))],
        out_specs=pl.BlockSpec((tc, d), lambda i: (i, 0)),
        out_shape=jax.ShapeDtypeStruct((n, d), F32),
        scratch_shapes=[pltpu.SMEM((GATHER_ROWS,), jnp.int32), pltpu.VMEM((GATHER_ROWS, d), F32),
                        pltpu.SemaphoreType.DMA(()), pltpu.SemaphoreType.DMA(())],
        compiler_params=_params("arbitrary"),
        name="combine",
    )(pos, y, x, wts)


def _moe(x, nw, whi, wlo, wg, wu, wd):
    n, d = x.shape
    ids, wts = _router(x, nw, whi, wlo)
    e_flat = ids[:, :TOP_K].reshape(-1)
    na = n * TOP_K
    onehot = (e_flat[:, None] == jnp.arange(N_EXPERTS, dtype=jnp.int32)[None, :]).astype(jnp.int32)
    csum = jnp.cumsum(onehot, axis=0)
    counts = csum[-1]
    rank = jnp.sum(csum * onehot, axis=1) - 1
    padded = (counts + FFN_TM - 1) // FFN_TM * FFN_TM
    pad_end = jnp.cumsum(padded)
    pad_off = pad_end - padded
    raw_off = jnp.cumsum(counts) - counts
    pos = jnp.sum(onehot * pad_off[None, :], axis=1) + rank
    n_rows = (na + N_EXPERTS * (FFN_TM - 1)) // FFN_TM * FFN_TM
    tile_start = jnp.arange(n_rows // FFN_TM, dtype=jnp.int32) * FFN_TM
    tile_expert = jnp.minimum(
        jnp.sum(tile_start[:, None] >= pad_end[None, :], axis=1), N_EXPERTS - 1).astype(jnp.int32)
    n_valid = (pad_end[-1] // FFN_TM).astype(jnp.int32).reshape(1)
    order = jnp.argsort(e_flat, stable=True).astype(jnp.int32)
    row = jnp.arange(n_rows, dtype=jnp.int32)
    e_row = jnp.repeat(tile_expert, FFN_TM)
    local = row - pad_off[e_row]
    src = jnp.clip(raw_off[e_row] + local, 0, na - 1)
    src_tok = jnp.where(local < counts[e_row], order[src] // TOP_K, 0).astype(jnp.int32)

    xs = _gather_rows(src_tok, x)
    y = _ffn(xs, nw, wg, wu, wd, tile_expert, n_valid, tm=FFN_TM, tf=512, residual=False)
    tc = GATHER_ROWS // TOP_K
    pos_tiles = pos.astype(jnp.int32).reshape(n // tc, tc, TOP_K).transpose(0, 2, 1).reshape(-1)
    return _combine(pos_tiles, y, x, wts)


def kernel(x_prompt, x_sample, norm_mix, w_in, lb_logits, hg_gnorm, na_qnorm, na_knorm, na_rpb, w_out, norm_ffn,
           ffn_gate, ffn_up, ffn_down, router, exp_gate, exp_up, exp_down):
    bp, tp, d = x_prompt.shape
    bs, ts, _ = x_sample.shape
    depth = w_in.shape[0]
    seq_lens = [tp] * bp + [ts] * bs
    assert tp % HG_TB == 0 and ts % HG_TB == 0 and min(tp, ts) // GRID_W >= WIN_H
    n = bp * tp + bs * ts
    assert n % FFN_TM == 0 and d % 2 == 0
    x = jnp.concatenate([x_prompt.reshape(bp * tp, d), x_sample.reshape(bs * ts, d)], axis=0)

    lb = jnp.cumsum(jax.nn.softmax(lb_logits.astype(F32), axis=1), axis=1)
    lb = lb - lb[:, :1]
    lbp = jnp.stack([jnp.log(lb), jnp.log1p(-lb)], axis=2)

    seg = jnp.asarray(np.kron(np.eye(MXU_N // NA_DH), np.ones((NA_DH, NA_DH))) / NA_DH, BF16)
    scale = NA_DH ** -0.5

    for l in range(depth):
        qkw = jnp.stack([jnp.tile(na_qnorm[l].astype(F32), NA_HEADS) * scale,
                         jnp.tile(na_knorm[l].astype(F32), NA_HEADS)])
        uhg, una = _inproj(x, norm_mix[l].reshape(1, d), w_in[l].astype(BF16), qkw, seg)
        o_f, o_b = _hgrn(uhg, lbp[0, l], lbp[1, l], seq_lens)
        o_na = _natten(una, _na_bias_table(na_rpb[l]), seq_lens)
        x = _outproj(o_f, o_b, uhg, o_na, x, hg_gnorm[l].reshape(1, HG_D), w_out[l].astype(BF16))
        nw = norm_ffn[l].reshape(1, d)
        j = l // 2
        if l % 2 == 0:
            n_tiles = n // TM
            x = _ffn(x, nw, ffn_gate[j:j + 1].astype(BF16), ffn_up[j:j + 1].astype(BF16),
                     ffn_down[j:j + 1].astype(BF16), jnp.zeros((n_tiles,), jnp.int32),
                     jnp.full((1,), n_tiles, jnp.int32), tm=TM, tf=ffn_gate.shape[2] // 2, residual=True)
        else:
            wr = jnp.pad(router[j].astype(F32), ((0, 0), (0, LANES - N_EXPERTS)))
            whi = wr.astype(BF16)
            wlo = (wr - whi.astype(F32)).astype(BF16)
            x = _moe(x, nw, whi, wlo, exp_gate[j].astype(BF16), exp_up[j].astype(BF16), exp_down[j].astype(BF16))

    y_prompt = x[:bp * tp].reshape(bp, tp, d)
    y_sample = x[bp * tp:].reshape(bs, ts, d)
    return (y_prompt, y_sample)
```

```python
import functools

import numpy as np
import jax
import jax.numpy as jnp
from jax import lax
from jax.experimental import pallas as pl
from jax.experimental.pallas import tpu as pltpu

F32 = jnp.float32
BF16 = jnp.bfloat16

GRID_W = 64
HG_HEADS, HG_D = 4, 128
HG_W = HG_HEADS * HG_D
NA_HEADS, NA_DH = 8, 64
NA_W = NA_HEADS * NA_DH
WIN_H, WIN_W = 8, 16
N_EXPERTS, TOP_K = 8, 2
EPS = 1e-6
NEG = -1e30

LANES = 128
MXU_N = 256
DMA_UNROLL = 8
VMEM_LIMIT = 48 * 1024 * 1024
TM = 512
HG_TB = 512
HG_C = 64
HG_MAX_EXPONENT = 80.0
NA_RB = 8
FFN_TM = 1024
FFN_SUB = 2 * MXU_N
FFN_VMEM_LIMIT = 54 * 1024 * 1024
GATHER_ROWS = 1024


def _dot(a, b):
    return jnp.dot(a, b, preferred_element_type=F32)


def _dot_nt(a, b):
    return lax.dot_general(a, b, (((1,), (1,)), ((), ())), preferred_element_type=F32)


def _dot_tn(a, b):
    return lax.dot_general(a, b, (((0,), (0,)), ((), ())), preferred_element_type=F32)


def _params(*sem):
    return pltpu.CompilerParams(dimension_semantics=sem, vmem_limit_bytes=VMEM_LIMIT)


def _resident(shape):
    nd = len(shape)
    return pl.BlockSpec(shape, lambda *_: (0,) * nd, pipeline_mode=pl.Buffered(1))


def _inproj_body(x_ref, nw_ref, w_ref, qkw_ref, seg_ref, uhg_ref, una_ref):
    x = x_ref[...]
    ms = jnp.mean(x * x, axis=-1, keepdims=True)
    h = (x * lax.rsqrt(ms + EPS) * nw_ref[...]).astype(BF16)
    n_hg = uhg_ref.shape[1]
    uhg_ref[...] = _dot(h, w_ref[:, 0:n_hg])
    seg = seg_ref[...]
    cw = seg.shape[0]
    for part in range(3):
        for c in range(0, NA_W, cw):
            col = part * NA_W + c
            y = _dot(h, w_ref[:, n_hg + col:n_hg + col + cw])
            if part < 2:
                msq = _dot((y * y).astype(BF16), seg)
                y = y * lax.rsqrt(msq + EPS) * qkw_ref[part:part + 1, c:c + cw]
            una_ref[:, col:col + cw] = y.astype(BF16)


def _inproj(x, nw, w, qkw, seg):
    n, d = x.shape
    n_hg = w.shape[1] - 3 * NA_W
    return pl.pallas_call(
        _inproj_body,
        grid=(n // TM,),
        in_specs=[
            pl.BlockSpec((TM, d), lambda i: (i, 0)),
            _resident(nw.shape), _resident(w.shape), _resident(qkw.shape), _resident(seg.shape),
        ],
        out_specs=[
            pl.BlockSpec((TM, n_hg), lambda i: (i, 0)),
            pl.BlockSpec((TM, 3 * NA_W), lambda i: (i, 0)),
        ],
        out_shape=[
            jax.ShapeDtypeStruct((n, n_hg), F32),
            jax.ShapeDtypeStruct((n, 3 * NA_W), BF16),
        ],
        compiler_params=_params("arbitrary"),
        name="inproj",
    )(x, nw, w, qkw, seg)


def _split2(x):
    hi = x.astype(BF16)
    lo = (x - hi.astype(F32)).astype(BF16)
    return hi, lo


def _hgrn_prep(q_ref, f_ref, lb_ref, tri_ref, qs_ref, ks_ref, bs_ref, r0, reverse):
    c = HG_C
    rows = pl.ds(r0, c)
    q = q_ref[rows, :]
    x = f_ref[rows, :]
    la = lb_ref[0:1, :]
    l1 = lb_ref[1:2, :]
    ls = jnp.minimum(x, 0.0) - jnp.log(1.0 + jnp.exp(-jnp.abs(x)))
    cc = l1 + ls
    g = jnp.maximum(la, cc) + jnp.log(1.0 + jnp.exp(-jnp.abs(la - cc)))
    bc = _dot(tri_ref[...], jnp.concatenate(_split2(g), axis=1))
    b = bc[:, :HG_W] + bc[:, HG_W:]
    qs_ref[rows, :] = q / (1.0 + jnp.exp(-q))
    ks_ref[rows, :] = jnp.exp(cc - x)
    bs_ref[rows, :] = b
    m = c // 2
    if reverse:
        near, total = b[m:m + 1, :], b[0:1, :]
    else:
        near, total = b[m - 1:m, :], b[c - 1:c, :]
    return jnp.maximum(-near, near - total)


def _hgrn_chunk(q_ref, k_ref, b_ref, v_ref, o_ref, st_ref, r0, reverse, fast):
    c = HG_C
    rows = pl.ds(r0, c)
    q = q_ref[rows, :]
    k = k_ref[rows, :]
    b = b_ref[rows, :]
    v = v_ref[rows, :]

    row = lax.broadcasted_iota(jnp.int32, (c, HG_D), 0)
    ri = lax.broadcasted_iota(jnp.int32, (c, c), 0)
    ci = lax.broadcasted_iota(jnp.int32, (c, c), 1)
    nb = c // 8
    i8 = lax.broadcasted_iota(jnp.int32, (nb, 8, HG_D), 1)

    for h in range(HG_HEADS):
        sl = slice(h * HG_D, (h + 1) * HG_D)
        bh, qh, kh, vh = b[:, sl], q[:, sl], k[:, sl], v[:, sl]
        vb = vh.astype(BF16)
        st = st_ref[h]
        btot = bh[0:1, :] if reverse else bh[c - 1:c, :]
        qe = (qh * jnp.exp(bh)).astype(BF16)
        ke = (kh * jnp.exp(btot - bh)).astype(BF16)
        o = _dot_nt(qe, st.astype(BF16))
        st_ref[h] = st * jnp.exp(btot) + _dot_tn(vb, ke)

        if fast:
            m = c // 2
            r = bh[m:m + 1, :] if reverse else bh[m - 1:m, :]
            qr = (qh * jnp.exp(bh - r)).astype(BF16)
            kr = (kh * jnp.exp(r - bh)).astype(BF16)
            causal = (ri <= ci) if reverse else (ri >= ci)
            a = jnp.where(causal, _dot_nt(qr, kr), 0.0)
            o_ref[rows, sl] = o + _dot(a.astype(BF16), vb)
            continue

        a = jnp.zeros((c, c), F32)
        w = c // 2
        while w >= 8:
            grp = c // (2 * w)
            b4 = bh.reshape(grp, 2 * w, HG_D)
            edge = w if reverse else w - 1
            r = jnp.broadcast_to(b4[:, edge:edge + 1, :], (grp, 2 * w, HG_D)).reshape(c, HG_D)
            e = jnp.exp(-jnp.abs(bh - r))
            late = (row & (2 * w - 1)) >= w
            is_q = jnp.logical_not(late) if reverse else late
            qs = jnp.where(is_q, qh * e, 0.0).astype(BF16)
            ks = jnp.where(is_q, 0.0, kh * e).astype(BF16)
            aw = _dot_nt(qs, ks)
            if grp > 1:
                sh = int(np.log2(2 * w))
                aw = jnp.where((ri >> sh) == (ci >> sh), aw, 0.0)
            a = a + aw
            w //= 2
        o = o + _dot(a.astype(BF16), vb)

        b3, q3, k3, v3 = (t.reshape(nb, 8, HG_D) for t in (bh, qh, kh, vh))
        od = jnp.zeros((nb, 8, HG_D), F32)
        for j in range(8):
            valid = (i8 <= j) if reverse else (i8 >= j)
            e = jnp.exp(jnp.where(valid, b3 - b3[:, j:j + 1, :], NEG))
            aj = jnp.sum(q3 * e * k3[:, j:j + 1, :], axis=-1, keepdims=True)
            od = od + aj * v3[:, j:j + 1, :]
        o_ref[rows, sl] = o + od.reshape(c, HG_D)


def _hgrn_body(fwd_ref, bwd_ref, first_ref, qf_ref, ff_ref, vf_ref, qb_ref, fb_ref, vb_ref,
               lbf_ref, lbb_ref, tril_ref, triu_ref, of_ref, ob_ref,
               stf_ref, stb_ref, qsf_ref, ksf_ref, bsf_ref, qsb_ref, ksb_ref, bsb_ref):
    del fwd_ref, bwd_ref
    g = pl.program_id(0)

    @pl.when(first_ref[g] == 1)
    def _():
        stf_ref[...] = jnp.zeros_like(stf_ref)
        stb_ref[...] = jnp.zeros_like(stb_ref)

    n_chunks = HG_TB // HG_C

    def prep(ci, worst):
        r0 = pl.multiple_of(ci * HG_C, HG_C)
        df = _hgrn_prep(qf_ref, ff_ref, lbf_ref, tril_ref, qsf_ref, ksf_ref, bsf_ref, r0, False)
        db = _hgrn_prep(qb_ref, fb_ref, lbb_ref, triu_ref, qsb_ref, ksb_ref, bsb_ref, r0, True)
        return jnp.maximum(worst, jnp.maximum(df, db))

    worst = jnp.max(lax.fori_loop(0, n_chunks, prep, jnp.zeros((1, HG_W), F32)))
    fits = worst <= HG_MAX_EXPONENT

    def run(fast):
        def body(ci, carry):
            rf = pl.multiple_of(ci * HG_C, HG_C)
            _hgrn_chunk(qsf_ref, ksf_ref, bsf_ref, vf_ref, of_ref, stf_ref, rf, False, fast)
            rb = pl.multiple_of((n_chunks - 1 - ci) * HG_C, HG_C)
            _hgrn_chunk(qsb_ref, ksb_ref, bsb_ref, vb_ref, ob_ref, stb_ref, rb, True, fast)
            return carry

        lax.fori_loop(0, n_chunks, body, 0, unroll=2 if fast else 1)

    @pl.when(fits)
    def _():
        run(True)

    @pl.when(jnp.logical_not(fits))
    def _():
        run(False)


def _hgrn(uhg, lbf, lbb, seq_lens):
    n = uhg.shape[0]
    fwd, bwd, first = [], [], []
    base = 0
    for t in seq_lens:
        nt = t // HG_TB
        for j in range(nt):
            fwd.append(base + j)
            bwd.append(base + nt - 1 - j)
            first.append(1 if j == 0 else 0)
        base += nt
    fwd, bwd, first = (jnp.asarray(np.array(a, np.int32)) for a in (fwd, bwd, first))
    tril = jnp.asarray(np.tril(np.ones((HG_C, HG_C), np.float32)), BF16)
    triu = jnp.asarray(np.triu(np.ones((HG_C, HG_C), np.float32)), BF16)

    def spec(which, col):
        if which == 0:
            return pl.BlockSpec((HG_TB, HG_W), lambda g, f, b, s: (f[g], col))
        return pl.BlockSpec((HG_TB, HG_W), lambda g, f, b, s: (b[g], col))

    const = lambda shape: pl.BlockSpec(shape, lambda g, f, b, s: (0, 0))
    grid_spec = pltpu.PrefetchScalarGridSpec(
        num_scalar_prefetch=3,
        grid=(n // HG_TB,),
        in_specs=[spec(0, 0), spec(0, 1), spec(0, 3), spec(1, 0), spec(1, 2), spec(1, 3),
                  const(lbf.shape), const(lbb.shape), const(tril.shape), const(triu.shape)],
        out_specs=[spec(0, 0), spec(1, 0)],
        scratch_shapes=[pltpu.VMEM((HG_HEADS, HG_D, HG_D), F32)] * 2 + [pltpu.VMEM((HG_TB, HG_W), F32)] * 6,
    )
    return pl.pallas_call(
        _hgrn_body,
        grid_spec=grid_spec,
        out_shape=[jax.ShapeDtypeStruct((n, HG_W), F32), jax.ShapeDtypeStruct((n, HG_W), F32)],
        compiler_params=_params("arbitrary"),
        name="hgrn",
    )(fwd, bwd, first, uhg, uhg, uhg, uhg, uhg, uhg, lbf, lbb, tril, triu)


def _na_body(prev_ref, next_ref, rloc_ref, rtot_ref, q_ref, kp_ref, kc_ref, kn_ref,
             vp_ref, vc_ref, vn_ref, bias_ref, o_ref, ks_ref, vs_ref):
    del prev_ref, next_ref
    g = pl.program_id(1)
    blk = NA_RB * GRID_W
    ks_ref[0:blk, :] = kp_ref[...]
    ks_ref[blk:2 * blk, :] = kc_ref[...]
    ks_ref[2 * blk:3 * blk, :] = kn_ref[...]
    vs_ref[0:blk, :] = vp_ref[...]
    vs_ref[blk:2 * blk, :] = vc_ref[...]
    vs_ref[2 * blk:3 * blk, :] = vn_ref[...]
    r0 = rloc_ref[g]
    n_rows = rtot_ref[g]
    first_head = lax.broadcasted_iota(jnp.int32, (GRID_W, LANES), 1) < NA_DH
    nkeys = WIN_H * GRID_W

    for rl in range(NA_RB):
        r = r0 + rl
        rs = jnp.clip(r - WIN_H // 2, 0, n_rows - WIN_H)
        off = pl.multiple_of((rs - r0 + NA_RB) * GRID_W, GRID_W)
        q2 = q_ref[rl * GRID_W:(rl + 1) * GRID_W, :]
        zero = jnp.zeros_like(q2)
        qq = jnp.concatenate([jnp.where(first_head, q2, zero), jnp.where(first_head, zero, q2)], axis=0)
        kw = ks_ref[pl.ds(off, nkeys), :]
        vw = vs_ref[pl.ds(off, nkeys), :]
        s = _dot_nt(qq, kw) + bias_ref[0, rs - r + WIN_H - 1]
        p = jnp.exp(s - jnp.max(s, axis=-1, keepdims=True))
        l = jnp.sum(p, axis=-1, keepdims=True)
        ov = _dot(p.astype(BF16), vw) / l
        out = jnp.where(first_head, ov[:GRID_W], ov[GRID_W:])
        o_ref[rl * GRID_W:(rl + 1) * GRID_W, :] = out.astype(o_ref.dtype)


def _natten(una, bias, seq_lens):
    n = una.shape[0]
    blk = NA_RB * GRID_W
    prev, nxt, rloc, rtot = [], [], [], []
    base = 0
    for t in seq_lens:
        rows = t // GRID_W
        nb = rows // NA_RB
        for j in range(nb):
            prev.append(base + max(j - 1, 0))
            nxt.append(base + min(j + 1, nb - 1))
            rloc.append(j * NA_RB)
            rtot.append(rows)
        base += nb
    prev, nxt, rloc, rtot = (jnp.asarray(np.array(a, np.int32)) for a in (prev, nxt, rloc, rtot))
    npair = NA_W // LANES

    def cur(part):
        return pl.BlockSpec((blk, LANES), lambda p, g, pv, nx, rl, rt: (g, part * npair + p))

    def nbr(part, which):
        if which == 0:
            return pl.BlockSpec((blk, LANES), lambda p, g, pv, nx, rl, rt: (pv[g], part * npair + p))
        return pl.BlockSpec((blk, LANES), lambda p, g, pv, nx, rl, rt: (nx[g], part * npair + p))

    grid_spec = pltpu.PrefetchScalarGridSpec(
        num_scalar_prefetch=4,
        grid=(npair, n // blk),
        in_specs=[cur(0), nbr(1, 0), cur(1), nbr(1, 1), nbr(2, 0), cur(2), nbr(2, 1),
                  pl.BlockSpec((1,) + bias.shape[1:], lambda p, g, pv, nx, rl, rt: (p, 0, 0, 0))],
        out_specs=pl.BlockSpec((blk, LANES), lambda p, g, pv, nx, rl, rt: (g, p)),
        scratch_shapes=[pltpu.VMEM((3 * blk, LANES), BF16), pltpu.VMEM((3 * blk, LANES), BF16)],
    )
    return pl.pallas_call(
        _na_body,
        grid_spec=grid_spec,
        out_shape=jax.ShapeDtypeStruct((n, NA_W), BF16),
        compiler_params=_params("arbitrary", "arbitrary"),
        name="natten",
    )(prev, nxt, rloc, rtot, una, una, una, una, una, una, una, bias)


def _na_bias_table(rpb):
    c = np.arange(GRID_W)
    cstart = np.clip(c - WIN_W // 2, 0, GRID_W - WIN_W)
    kc = np.arange(GRID_W)
    mask = (kc[None, :] >= cstart[:, None]) & (kc[None, :] < cstart[:, None] + WIN_W)
    dx = np.clip(kc[None, :] - c[:, None] + WIN_W - 1, 0, 2 * WIN_W - 2)
    onehot = np.zeros((2 * WIN_W - 1, GRID_W * GRID_W), np.float32)
    onehot[dx.reshape(-1), np.arange(GRID_W * GRID_W)] = mask.reshape(-1)
    t = jnp.einsum('hyx,xq->hyq', rpb.astype(F32), jnp.asarray(onehot), precision=lax.Precision.HIGHEST)
    t = t + jnp.asarray(np.where(mask.reshape(-1), 0.0, NEG).astype(np.float32))
    t = jnp.stack([t[:, o:o + WIN_H] for o in range(WIN_H)], axis=1)
    t = t.reshape(NA_HEADS, WIN_H, WIN_H, GRID_W, GRID_W)
    t = t.transpose(0, 1, 3, 2, 4).reshape(NA_HEADS, WIN_H, GRID_W, WIN_H * GRID_W)
    t = t.reshape(NA_HEADS // 2, 2, WIN_H, GRID_W, WIN_H * GRID_W).transpose(0, 2, 1, 3, 4)
    return t.reshape(NA_HEADS // 2, WIN_H, 2 * GRID_W, WIN_H * GRID_W)


def _outproj_body(of_ref, ob_ref, g_ref, ona_ref, x_ref, gw_ref, w_ref, xo_ref):
    o = of_ref[...] + ob_ref[...]
    g = g_ref[...]
    gate = g / (1.0 + jnp.exp(-g))
    parts = []
    for h in range(HG_HEADS):
        sl = slice(h * HG_D, (h + 1) * HG_D)
        oh = o[:, sl]
        ms = jnp.mean(oh * oh, axis=-1, keepdims=True)
        parts.append(oh * lax.rsqrt(ms + EPS) * gw_ref[...] * gate[:, sl])
    hg = jnp.concatenate(parts, axis=1).astype(BF16)
    y = _dot(hg, w_ref[0:HG_W, :]) + _dot(ona_ref[...], w_ref[HG_W:, :])
    xo_ref[...] = x_ref[...] + y


def _outproj(o_f, o_b, uhg, o_na, x, gw, w):
    n, d = x.shape
    tok = lambda width, col=0: pl.BlockSpec((TM, width), lambda i: (i, col))
    return pl.pallas_call(
        _outproj_body,
        grid=(n // TM,),
        in_specs=[tok(HG_W), tok(HG_W), tok(HG_W, 4), tok(NA_W), tok(d),
                  _resident(gw.shape), _resident(w.shape)],
        out_specs=tok(d),
        out_shape=jax.ShapeDtypeStruct((n, d), F32),
        compiler_params=_params("arbitrary"),
        name="outproj",
    )(o_f, o_b, uhg, o_na, x, gw, w)


def _ffn_body(te_ref, nv_ref, x_ref, nw_ref, wg_ref, wu_ref, wd_ref, o_ref, xn_ref, *, residual):
    del te_ref
    i = pl.program_id(0)
    f = pl.program_id(1)
    valid = i < nv_ref[0]

    @pl.when(jnp.logical_and(f == 0, valid))
    def _():
        xr = x_ref[...]
        ms = jnp.mean(xr * xr, axis=-1, keepdims=True)
        xn_ref[...] = (xr * lax.rsqrt(ms + EPS) * nw_ref[...]).astype(BF16)
        o_ref[...] = xr if residual else jnp.zeros_like(xr)

    @pl.when(jnp.logical_and(f == 0, jnp.logical_not(valid)))
    def _():
        o_ref[...] = jnp.zeros_like(o_ref)

    @pl.when(valid)
    def _():
        x = xn_ref[...]
        tf = wg_ref.shape[2]
        for c0 in range(0, tf, FFN_SUB):
            c1 = min(c0 + FFN_SUB, tf)
            gate = _dot(x, wg_ref[0, :, c0:c1])
            up = _dot(x, wu_ref[0, :, c0:c1])
            hid = (gate / (1.0 + jnp.exp(-gate)) * up).astype(BF16)
            o_ref[...] += _dot(hid, wd_ref[0, c0:c1, :])


def _ffn(x, nw, wg, wu, wd, tile_expert, n_valid, tm, tf, residual):
    p, d = x.shape
    ff = wg.shape[2]
    assert ff % tf == 0 and tf % MXU_N == 0
    once = dict(pipeline_mode=pl.Buffered(1)) if (wg.shape[0] == 1 and tf == ff) else {}
    row = pl.BlockSpec((tm, d), lambda i, f, te, nv: (i, 0))
    grid_spec = pltpu.PrefetchScalarGridSpec(
        num_scalar_prefetch=2,
        grid=(p // tm, ff // tf),
        in_specs=[row,
                  pl.BlockSpec(nw.shape, lambda i, f, te, nv: (0, 0)),
                  pl.BlockSpec((1, d, tf), lambda i, f, te, nv: (te[i], 0, f), **once),
                  pl.BlockSpec((1, d, tf), lambda i, f, te, nv: (te[i], 0, f), **once),
                  pl.BlockSpec((1, tf, d), lambda i, f, te, nv: (te[i], f, 0), **once)],
        out_specs=row,
        scratch_shapes=[pltpu.VMEM((tm, d), BF16)],
    )
    return pl.pallas_call(
        functools.partial(_ffn_body, residual=residual),
        grid_spec=grid_spec,
        out_shape=jax.ShapeDtypeStruct((p, d), F32),
        compiler_params=pltpu.CompilerParams(dimension_semantics=("arbitrary", "arbitrary"),
                                             vmem_limit_bytes=FFN_VMEM_LIMIT),
        name="ffn_res" if residual else "ffn_grouped",
    )(tile_expert, n_valid, x, nw, wg, wu, wd)


def _router_body(x_ref, nw_ref, whi_ref, wlo_ref, ids_ref, wts_ref):
    x = x_ref[...]
    ms = jnp.mean(x * x, axis=-1, keepdims=True)
    h = x * lax.rsqrt(ms + EPS) * nw_ref[...]
    hh = h.astype(BF16)
    hl = (h - hh.astype(F32)).astype(BF16)
    logits = _dot(hh, whi_ref[...]) + (_dot(hl, whi_ref[...]) + _dot(hh, wlo_ref[...]))
    col = lax.broadcasted_iota(jnp.int32, logits.shape, 1).astype(F32)
    logits = jnp.where(col < N_EXPERTS, logits, -jnp.inf)
    m1 = jnp.max(logits, axis=-1, keepdims=True)
    i1 = jnp.min(jnp.where(logits == m1, col, float(LANES)), axis=-1, keepdims=True)
    rest = jnp.where(col == i1, -jnp.inf, logits)
    m2 = jnp.max(rest, axis=-1, keepdims=True)
    i2 = jnp.min(jnp.where(rest == m2, col, float(LANES)), axis=-1, keepdims=True)
    e = jnp.exp(m2 - m1)
    w1 = 1.0 / (1.0 + e)
    w2 = e / (1.0 + e)
    ids_ref[...] = jnp.where(col == 0.0, i1, jnp.where(col == 1.0, i2, 0.0)).astype(jnp.int32)
    wts_ref[...] = jnp.where(col == 0.0, w1, jnp.where(col == 1.0, w2, 0.0))


def _router(x, nw, whi, wlo):
    n, d = x.shape
    return pl.pallas_call(
        _router_body,
        grid=(n // TM,),
        in_specs=[pl.BlockSpec((TM, d), lambda i: (i, 0)),
                  _resident(nw.shape), _resident(whi.shape), _resident(wlo.shape)],
        out_specs=[pl.BlockSpec((TM, LANES), lambda i: (i, 0)), pl.BlockSpec((TM, LANES), lambda i: (i, 0))],
        out_shape=[jax.ShapeDtypeStruct((n, LANES), jnp.int32), jax.ShapeDtypeStruct((n, LANES), F32)],
        compiler_params=_params("arbitrary"),
        name="router",
    )(x, nw, whi, wlo)


def _row_copy(src_ref, dst_ref, sem, src_row, dst_row):
    return pltpu.make_async_copy(src_ref.at[pl.ds(src_row, 1)], dst_ref.at[pl.ds(dst_row, 1)], sem)


def _gather_into(idx_smem, src_hbm, dst_ref, sem):
    def start(b, carry):
        for u in range(DMA_UNROLL):
            r = b * DMA_UNROLL + u
            _row_copy(src_hbm, dst_ref, sem, idx_smem[r], r).start(priority=u % 2)
        return carry

    lax.fori_loop(0, GATHER_ROWS // DMA_UNROLL, start, 0)

    def wait(b, carry):
        for u in range(DMA_UNROLL):
            _row_copy(src_hbm, dst_ref, sem, 0, b * DMA_UNROLL + u).wait()
        return carry

    lax.fori_loop(0, GATHER_ROWS // DMA_UNROLL, wait, 0)


def _gather_body(idx_hbm, src_hbm, o_ref, idx_smem, sem_idx, sem_rows):
    i = pl.program_id(0)
    cp = pltpu.make_async_copy(idx_hbm.at[pl.ds(i * GATHER_ROWS, GATHER_ROWS)], idx_smem, sem_idx)
    cp.start()
    cp.wait()
    _gather_into(idx_smem, src_hbm, o_ref, sem_rows)


def _gather_rows(idx, src):
    p = idx.shape[0]
    w = src.shape[1]
    return pl.pallas_call(
        _gather_body,
        grid=(p // GATHER_ROWS,),
        in_specs=[pl.BlockSpec(memory_space=pl.ANY), pl.BlockSpec(memory_space=pl.ANY)],
        out_specs=pl.BlockSpec((GATHER_ROWS, w), lambda i: (i, 0)),
        out_shape=jax.ShapeDtypeStruct((p, w), src.dtype),
        scratch_shapes=[pltpu.SMEM((GATHER_ROWS,), jnp.int32),
                        pltpu.SemaphoreType.DMA(()), pltpu.SemaphoreType.DMA(())],
        compiler_params=_params("arbitrary"),
        name="gather_rows",
    )(idx, src)


def _combine_body(pos_hbm, y_hbm, x_ref, wts_ref, o_ref, pos_smem, buf_ref, sem_idx, sem_rows):
    i = pl.program_id(0)
    tc = GATHER_ROWS // TOP_K
    cp = pltpu.make_async_copy(pos_hbm.at[pl.ds(i * GATHER_ROWS, GATHER_ROWS)], pos_smem, sem_idx)
    cp.start()
    cp.wait()
    _gather_into(pos_smem, y_hbm, buf_ref, sem_rows)
    w = wts_ref[...]
    o_ref[...] = x_ref[...] + (w[:, 0:1] * buf_ref[0:tc, :] + w[:, 1:2] * buf_ref[tc:2 * tc, :])


def _combine(pos, y, x, wts):
    n, d = x.shape
    tc = GATHER_ROWS // TOP_K
    return pl.pallas_call(
        _combine_body,
        grid=(n // tc,),
        in_specs=[pl.BlockSpec(memory_space=pl.ANY), pl.BlockSpec(memory_space=pl.ANY),
                  pl.BlockSpec((tc, d), lambda i: (i, 0)), pl.BlockSpec((tc, LANES), lambda i: (i, 0))],
        out_specs=pl.BlockSpec((tc, d), lambda i: (i, 0)),
        out_shape=jax.ShapeDtypeStruct((n, d), F32),
        scratch_shapes=[pltpu.SMEM((GATHER_ROWS,), jnp.int32), pltpu.VMEM((GATHER_ROWS, d), F32),
                        pltpu.SemaphoreType.DMA(()), pltpu.SemaphoreType.DMA(())],
        compiler_params=_params("arbitrary"),
        name="combine",
    )(pos, y, x, wts)


def _moe(x, nw, whi, wlo, wg, wu, wd):
    n, d = x.shape
    ids, wts = _router(x, nw, whi, wlo)
    e_flat = ids[:, :TOP_K].reshape(-1)
    na = n * TOP_K
    onehot = (e_flat[:, None] == jnp.arange(N_EXPERTS, dtype=jnp.int32)[None, :]).astype(jnp.int32)
    csum = jnp.cumsum(onehot, axis=0)
    counts = csum[-1]
    rank = jnp.sum(csum * onehot, axis=1) - 1
    padded = (counts + FFN_TM - 1) // FFN_TM * FFN_TM
    pad_end = jnp.cumsum(padded)
    pad_off = pad_end - padded
    raw_off = jnp.cumsum(counts) - counts
    pos = jnp.sum(onehot * pad_off[None, :], axis=1) + rank
    n_rows = (na + N_EXPERTS * (FFN_TM - 1)) // FFN_TM * FFN_TM
    tile_start = jnp.arange(n_rows // FFN_TM, dtype=jnp.int32) * FFN_TM
    tile_expert = jnp.minimum(
        jnp.sum(tile_start[:, None] >= pad_end[None, :], axis=1), N_EXPERTS - 1).astype(jnp.int32)
    n_valid = (pad_end[-1] // FFN_TM).astype(jnp.int32).reshape(1)
    order = jnp.argsort(e_flat, stable=True).astype(jnp.int32)
    row = jnp.arange(n_rows, dtype=jnp.int32)
    e_row = jnp.repeat(tile_expert, FFN_TM)
    local = row - pad_off[e_row]
    src = jnp.clip(raw_off[e_row] + local, 0, na - 1)
    src_tok = jnp.where(local < counts[e_row], order[src] // TOP_K, 0).astype(jnp.int32)

    xs = _gather_rows(src_tok, x)
    y = _ffn(xs, nw, wg, wu, wd, tile_expert, n_valid, tm=FFN_TM, tf=wg.shape[2] // 2, residual=False)
    tc = GATHER_ROWS // TOP_K
    pos_tiles = pos.astype(jnp.int32).reshape(n // tc, tc, TOP_K).transpose(0, 2, 1).reshape(-1)
    return _combine(pos_tiles, y, x, wts)


def kernel(x_prompt, x_sample, norm_mix, w_in, lb_logits, hg_gnorm, na_qnorm, na_knorm, na_rpb, w_out, norm_ffn,
           ffn_gate, ffn_up, ffn_down, router, exp_gate, exp_up, exp_down):
    bp, tp, d = x_prompt.shape
    bs, ts, _ = x_sample.shape
    depth = w_in.shape[0]
    seq_lens = [tp] * bp + [ts] * bs
    assert tp % HG_TB == 0 and ts % HG_TB == 0 and min(tp, ts) // GRID_W >= WIN_H
    n = bp * tp + bs * ts
    assert n % FFN_TM == 0 and d % 2 == 0
    x = jnp.concatenate([x_prompt.reshape(bp * tp, d), x_sample.reshape(bs * ts, d)], axis=0)

    lb = jnp.cumsum(jax.nn.softmax(lb_logits.astype(F32), axis=1), axis=1)
    lb = lb - lb[:, :1]
    lbp = jnp.stack([jnp.log(lb), jnp.log1p(-lb)], axis=2)

    seg = jnp.asarray(np.kron(np.eye(MXU_N // NA_DH), np.ones((NA_DH, NA_DH))) / NA_DH, BF16)
    scale = NA_DH ** -0.5

    for l in range(depth):
        qkw = jnp.stack([jnp.tile(na_qnorm[l].astype(F32), NA_HEADS) * scale,
                         jnp.tile(na_knorm[l].astype(F32), NA_HEADS)])
        uhg, una = _inproj(x, norm_mix[l].reshape(1, d), w_in[l].astype(BF16), qkw, seg)
        o_f, o_b = _hgrn(uhg, lbp[0, l], lbp[1, l], seq_lens)
        o_na = _natten(una, _na_bias_table(na_rpb[l]), seq_lens)
        x = _outproj(o_f, o_b, uhg, o_na, x, hg_gnorm[l].reshape(1, HG_D), w_out[l].astype(BF16))
        nw = norm_ffn[l].reshape(1, d)
        j = l // 2
        if l % 2 == 0:
            n_tiles = n // FFN_TM
            x = _ffn(x, nw, ffn_gate[j:j + 1].astype(BF16), ffn_up[j:j + 1].astype(BF16),
                     ffn_down[j:j + 1].astype(BF16), jnp.zeros((n_tiles,), jnp.int32),
                     jnp.full((1,), n_tiles, jnp.int32), tm=FFN_TM, tf=ffn_gate.shape[2], residual=True)
        else:
            wr = jnp.pad(router[j].astype(F32), ((0, 0), (0, LANES - N_EXPERTS)))
            whi = wr.astype(BF16)
            wlo = (wr - whi.astype(F32)).astype(BF16)
            x = _moe(x, nw, whi, wlo, exp_gate[j].astype(BF16), exp_up[j].astype(BF16), exp_down[j].astype(BF16))

    y_prompt = x[:bp * tp].reshape(bp, tp, d)
    y_sample = x[bp * tp:].reshape(bs, ts, d)
    return (y_prompt, y_sample)
```

```python
import functools

import numpy as np
import jax
import jax.numpy as jnp
from jax import lax
from jax.experimental import pallas as pl
from jax.experimental.pallas import tpu as pltpu

F32 = jnp.float32
BF16 = jnp.bfloat16

GRID_W = 64
HG_HEADS, HG_D = 4, 128
HG_W = HG_HEADS * HG_D
NA_HEADS, NA_DH = 8, 64
NA_W = NA_HEADS * NA_DH
WIN_H, WIN_W = 8, 16
N_EXPERTS, TOP_K = 8, 2
EPS = 1e-6
NEG = -1e30

LANES = 128
MXU_N = 256
DMA_UNROLL = 8
VMEM_LIMIT = 48 * 1024 * 1024
TM = 512
HG_TB = 512
HG_C = 128
HG_MAX_EXPONENT = 80.0
NA_RB = 8
FFN_TM = 1024
FFN_SUB = 2 * MXU_N
FFN_VMEM_LIMIT = 54 * 1024 * 1024
GATHER_ROWS = 1024


def _dot(a, b):
    return jnp.dot(a, b, preferred_element_type=F32)


def _dot_nt(a, b):
    return lax.dot_general(a, b, (((1,), (1,)), ((), ())), preferred_element_type=F32)


def _dot_tn(a, b):
    return lax.dot_general(a, b, (((0,), (0,)), ((), ())), preferred_element_type=F32)


def _params(*sem):
    return pltpu.CompilerParams(dimension_semantics=sem, vmem_limit_bytes=VMEM_LIMIT)


def _resident(shape):
    nd = len(shape)
    return pl.BlockSpec(shape, lambda *_: (0,) * nd, pipeline_mode=pl.Buffered(1))


def _inproj_body(x_ref, nw_ref, w_ref, qkw_ref, seg_ref, uhg_ref, una_ref):
    x = x_ref[...]
    ms = jnp.mean(x * x, axis=-1, keepdims=True)
    h = (x * lax.rsqrt(ms + EPS) * nw_ref[...]).astype(BF16)
    n_hg = uhg_ref.shape[1]
    uhg_ref[...] = _dot(h, w_ref[:, 0:n_hg])
    una_ref[:, 2 * NA_W:] = _dot(h, w_ref[:, n_hg + 2 * NA_W:]).astype(BF16)
    seg = seg_ref[...]
    cw = seg.shape[0]
    cols = [(part, c) for part in range(2) for c in range(0, NA_W, cw)]
    ys = [_dot(h, w_ref[:, n_hg + part * NA_W + c:n_hg + part * NA_W + c + cw]) for part, c in cols]
    for (part, c), y in zip(cols, ys):
        msq = _dot((y * y).astype(BF16), seg)
        y = y * lax.rsqrt(msq + EPS) * qkw_ref[part:part + 1, c:c + cw]
        una_ref[:, part * NA_W + c:part * NA_W + c + cw] = y.astype(BF16)


def _inproj(x, nw, w, qkw, seg):
    n, d = x.shape
    n_hg = w.shape[1] - 3 * NA_W
    return pl.pallas_call(
        _inproj_body,
        grid=(n // TM,),
        in_specs=[
            pl.BlockSpec((TM, d), lambda i: (i, 0)),
            _resident(nw.shape), _resident(w.shape), _resident(qkw.shape), _resident(seg.shape),
        ],
        out_specs=[
            pl.BlockSpec((TM, n_hg), lambda i: (i, 0)),
            pl.BlockSpec((TM, 3 * NA_W), lambda i: (i, 0)),
        ],
        out_shape=[
            jax.ShapeDtypeStruct((n, n_hg), F32),
            jax.ShapeDtypeStruct((n, 3 * NA_W), BF16),
        ],
        compiler_params=_params("arbitrary"),
        name="inproj",
    )(x, nw, w, qkw, seg)


def _split2(x):
    hi = x.astype(BF16)
    lo = (x - hi.astype(F32)).astype(BF16)
    return hi, lo


def _hgrn_prep(q_ref, f_ref, lb_ref, tri_ref, qs_ref, ks_ref, bs_ref, r0, reverse):
    c = HG_C
    rows = pl.ds(r0, c)
    q = q_ref[rows, :]
    x = f_ref[rows, :]
    la = lb_ref[0:1, :]
    l1 = lb_ref[1:2, :]
    ls = jnp.minimum(x, 0.0) - jnp.log(1.0 + jnp.exp(-jnp.abs(x)))
    cc = l1 + ls
    g = jnp.maximum(la, cc) + jnp.log(1.0 + jnp.exp(-jnp.abs(la - cc)))
    bc = _dot(tri_ref[...], jnp.concatenate(_split2(g), axis=1))
    b = bc[:, :HG_W] + bc[:, HG_W:]
    qs_ref[rows, :] = q / (1.0 + jnp.exp(-q))
    ks_ref[rows, :] = jnp.exp(cc - x)
    bs_ref[rows, :] = b
    m = c // 2
    if reverse:
        near, total = b[m:m + 1, :], b[0:1, :]
    else:
        near, total = b[m - 1:m, :], b[c - 1:c, :]
    return jnp.maximum(-near, near - total)


def _hgrn_chunk(q_ref, k_ref, b_ref, v_ref, o_ref, st_ref, r0, reverse, fast):
    c = HG_C
    rows = pl.ds(r0, c)
    q = q_ref[rows, :]
    k = k_ref[rows, :]
    b = b_ref[rows, :]
    v = v_ref[rows, :]

    row = lax.broadcasted_iota(jnp.int32, (c, HG_D), 0)
    ri = lax.broadcasted_iota(jnp.int32, (c, c), 0)
    ci = lax.broadcasted_iota(jnp.int32, (c, c), 1)
    nb = c // 8
    i8 = lax.broadcasted_iota(jnp.int32, (nb, 8, HG_D), 1)

    for h in range(HG_HEADS):
        sl = slice(h * HG_D, (h + 1) * HG_D)
        bh, qh, kh, vh = b[:, sl], q[:, sl], k[:, sl], v[:, sl]
        vb = vh.astype(BF16)
        st = st_ref[h]
        btot = bh[0:1, :] if reverse else bh[c - 1:c, :]
        qe = (qh * jnp.exp(bh)).astype(BF16)
        ke = (kh * jnp.exp(btot - bh)).astype(BF16)
        o = _dot_nt(qe, st.astype(BF16))
        st_ref[h] = st * jnp.exp(btot) + _dot_tn(vb, ke)

        if fast:
            m = c // 2
            r = bh[m:m + 1, :] if reverse else bh[m - 1:m, :]
            qr = (qh * jnp.exp(bh - r)).astype(BF16)
            kr = (kh * jnp.exp(r - bh)).astype(BF16)
            causal = (ri <= ci) if reverse else (ri >= ci)
            a = jnp.where(causal, _dot_nt(qr, kr), 0.0)
            o_ref[rows, sl] = o + _dot(a.astype(BF16), vb)
            continue

        a = jnp.zeros((c, c), F32)
        w = c // 2
        while w >= 8:
            grp = c // (2 * w)
            b4 = bh.reshape(grp, 2 * w, HG_D)
            edge = w if reverse else w - 1
            r = jnp.broadcast_to(b4[:, edge:edge + 1, :], (grp, 2 * w, HG_D)).reshape(c, HG_D)
            e = jnp.exp(-jnp.abs(bh - r))
            late = (row & (2 * w - 1)) >= w
            is_q = jnp.logical_not(late) if reverse else late
            qs = jnp.where(is_q, qh * e, 0.0).astype(BF16)
            ks = jnp.where(is_q, 0.0, kh * e).astype(BF16)
            aw = _dot_nt(qs, ks)
            if grp > 1:
                sh = int(np.log2(2 * w))
                aw = jnp.where((ri >> sh) == (ci >> sh), aw, 0.0)
            a = a + aw
            w //= 2
        o = o + _dot(a.astype(BF16), vb)

        b3, q3, k3, v3 = (t.reshape(nb, 8, HG_D) for t in (bh, qh, kh, vh))
        od = jnp.zeros((nb, 8, HG_D), F32)
        for j in range(8):
            valid = (i8 <= j) if reverse else (i8 >= j)
            e = jnp.exp(jnp.where(valid, b3 - b3[:, j:j + 1, :], NEG))
            aj = jnp.sum(q3 * e * k3[:, j:j + 1, :], axis=-1, keepdims=True)
            od = od + aj * v3[:, j:j + 1, :]
        o_ref[rows, sl] = o + od.reshape(c, HG_D)


def _hgrn_body(fwd_ref, bwd_ref, first_ref, qf_ref, ff_ref, vf_ref, qb_ref, fb_ref, vb_ref,
               lbf_ref, lbb_ref, tril_ref, triu_ref, of_ref, ob_ref,
               stf_ref, stb_ref, qsf_ref, ksf_ref, bsf_ref, qsb_ref, ksb_ref, bsb_ref):
    del fwd_ref, bwd_ref
    g = pl.program_id(0)

    @pl.when(first_ref[g] == 1)
    def _():
        stf_ref[...] = jnp.zeros_like(stf_ref)
        stb_ref[...] = jnp.zeros_like(stb_ref)

    n_chunks = HG_TB // HG_C

    def prep(ci, worst):
        r0 = pl.multiple_of(ci * HG_C, HG_C)
        df = _hgrn_prep(qf_ref, ff_ref, lbf_ref, tril_ref, qsf_ref, ksf_ref, bsf_ref, r0, False)
        db = _hgrn_prep(qb_ref, fb_ref, lbb_ref, triu_ref, qsb_ref, ksb_ref, bsb_ref, r0, True)
        return jnp.maximum(worst, jnp.maximum(df, db))

    worst = jnp.max(lax.fori_loop(0, n_chunks, prep, jnp.zeros((1, HG_W), F32)))
    fits = worst <= HG_MAX_EXPONENT

    def run(fast):
        def body(ci, carry):
            rf = pl.multiple_of(ci * HG_C, HG_C)
            _hgrn_chunk(qsf_ref, ksf_ref, bsf_ref, vf_ref, of_ref, stf_ref, rf, False, fast)
            rb = pl.multiple_of((n_chunks - 1 - ci) * HG_C, HG_C)
            _hgrn_chunk(qsb_ref, ksb_ref, bsb_ref, vb_ref, ob_ref, stb_ref, rb, True, fast)
            return carry

        lax.fori_loop(0, n_chunks, body, 0, unroll=2 if fast else 1)

    @pl.when(fits)
    def _():
        run(True)

    @pl.when(jnp.logical_not(fits))
    def _():
        run(False)


def _hgrn(uhg, lbf, lbb, seq_lens):
    n = uhg.shape[0]
    fwd, bwd, first = [], [], []
    base = 0
    for t in seq_lens:
        nt = t // HG_TB
        for j in range(nt):
            fwd.append(base + j)
            bwd.append(base + nt - 1 - j)
            first.append(1 if j == 0 else 0)
        base += nt
    fwd, bwd, first = (jnp.asarray(np.array(a, np.int32)) for a in (fwd, bwd, first))
    tril = jnp.asarray(np.tril(np.ones((HG_C, HG_C), np.float32)), BF16)
    triu = jnp.asarray(np.triu(np.ones((HG_C, HG_C), np.float32)), BF16)

    def spec(which, col):
        if which == 0:
            return pl.BlockSpec((HG_TB, HG_W), lambda g, f, b, s: (f[g], col))
        return pl.BlockSpec((HG_TB, HG_W), lambda g, f, b, s: (b[g], col))

    const = lambda shape: pl.BlockSpec(shape, lambda g, f, b, s: (0, 0))
    grid_spec = pltpu.PrefetchScalarGridSpec(
        num_scalar_prefetch=3,
        grid=(n // HG_TB,),
        in_specs=[spec(0, 0), spec(0, 1), spec(0, 3), spec(1, 0), spec(1, 2), spec(1, 3),
                  const(lbf.shape), const(lbb.shape), const(tril.shape), const(triu.shape)],
        out_specs=[spec(0, 0), spec(1, 0)],
        scratch_shapes=[pltpu.VMEM((HG_HEADS, HG_D, HG_D), F32)] * 2 + [pltpu.VMEM((HG_TB, HG_W), F32)] * 6,
    )
    return pl.pallas_call(
        _hgrn_body,
        grid_spec=grid_spec,
        out_shape=[jax.ShapeDtypeStruct((n, HG_W), F32), jax.ShapeDtypeStruct((n, HG_W), F32)],
        compiler_params=_params("arbitrary"),
        name="hgrn",
    )(fwd, bwd, first, uhg, uhg, uhg, uhg, uhg, uhg, lbf, lbb, tril, triu)


def _na_body(prev_ref, next_ref, rloc_ref, rtot_ref, q_ref, kp_ref, kc_ref, kn_ref,
             vp_ref, vc_ref, vn_ref, bias_ref, o_ref, ks_ref, vs_ref):
    del prev_ref, next_ref
    g = pl.program_id(1)
    blk = NA_RB * GRID_W
    ks_ref[0:blk, :] = kp_ref[...]
    ks_ref[blk:2 * blk, :] = kc_ref[...]
    ks_ref[2 * blk:3 * blk, :] = kn_ref[...]
    vs_ref[0:blk, :] = vp_ref[...]
    vs_ref[blk:2 * blk, :] = vc_ref[...]
    vs_ref[2 * blk:3 * blk, :] = vn_ref[...]
    r0 = rloc_ref[g]
    n_rows = rtot_ref[g]
    first_head = lax.broadcasted_iota(jnp.int32, (GRID_W, LANES), 1) < NA_DH
    nkeys = WIN_H * GRID_W

    for rl in range(NA_RB):
        r = r0 + rl
        rs = jnp.clip(r - WIN_H // 2, 0, n_rows - WIN_H)
        off = pl.multiple_of((rs - r0 + NA_RB) * GRID_W, GRID_W)
        q2 = q_ref[rl * GRID_W:(rl + 1) * GRID_W, :]
        zero = jnp.zeros_like(q2)
        qq = jnp.concatenate([jnp.where(first_head, q2, zero), jnp.where(first_head, zero, q2)], axis=0)
        kw = ks_ref[pl.ds(off, nkeys), :]
        vw = vs_ref[pl.ds(off, nkeys), :]
        s = _dot_nt(qq, kw) + bias_ref[0, rs - r + WIN_H - 1]
        p = jnp.exp(s - jnp.max(s, axis=-1, keepdims=True))
        l = jnp.sum(p, axis=-1, keepdims=True)
        ov = _dot(p.astype(BF16), vw) / l
        out = jnp.where(first_head, ov[:GRID_W], ov[GRID_W:])
        o_ref[rl * GRID_W:(rl + 1) * GRID_W, :] = out.astype(o_ref.dtype)


def _natten(una, bias, seq_lens):
    n = una.shape[0]
    blk = NA_RB * GRID_W
    prev, nxt, rloc, rtot = [], [], [], []
    base = 0
    for t in seq_lens:
        rows = t // GRID_W
        nb = rows // NA_RB
        for j in range(nb):
            prev.append(base + max(j - 1, 0))
            nxt.append(base + min(j + 1, nb - 1))
            rloc.append(j * NA_RB)
            rtot.append(rows)
        base += nb
    prev, nxt, rloc, rtot = (jnp.asarray(np.array(a, np.int32)) for a in (prev, nxt, rloc, rtot))
    npair = NA_W // LANES

    def cur(part):
        return pl.BlockSpec((blk, LANES), lambda p, g, pv, nx, rl, rt: (g, part * npair + p))

    def nbr(part, which):
        if which == 0:
            return pl.BlockSpec((blk, LANES), lambda p, g, pv, nx, rl, rt: (pv[g], part * npair + p))
        return pl.BlockSpec((blk, LANES), lambda p, g, pv, nx, rl, rt: (nx[g], part * npair + p))

    grid_spec = pltpu.PrefetchScalarGridSpec(
        num_scalar_prefetch=4,
        grid=(npair, n // blk),
        in_specs=[cur(0), nbr(1, 0), cur(1), nbr(1, 1), nbr(2, 0), cur(2), nbr(2, 1),
                  pl.BlockSpec((1,) + bias.shape[1:], lambda p, g, pv, nx, rl, rt: (p, 0, 0, 0))],
        out_specs=pl.BlockSpec((blk, LANES), lambda p, g, pv, nx, rl, rt: (g, p)),
        scratch_shapes=[pltpu.VMEM((3 * blk, LANES), BF16), pltpu.VMEM((3 * blk, LANES), BF16)],
    )
    return pl.pallas_call(
        _na_body,
        grid_spec=grid_spec,
        out_shape=jax.ShapeDtypeStruct((n, NA_W), BF16),
        compiler_params=_params("arbitrary", "arbitrary"),
        name="natten",
    )(prev, nxt, rloc, rtot, una, una, una, una, una, una, una, bias)


def _na_bias_table(rpb):
    c = np.arange(GRID_W)
    cstart = np.clip(c - WIN_W // 2, 0, GRID_W - WIN_W)
    kc = np.arange(GRID_W)
    mask = (kc[None, :] >= cstart[:, None]) & (kc[None, :] < cstart[:, None] + WIN_W)
    dx = np.clip(kc[None, :] - c[:, None] + WIN_W - 1, 0, 2 * WIN_W - 2)
    onehot = np.zeros((2 * WIN_W - 1, GRID_W * GRID_W), np.float32)
    onehot[dx.reshape(-1), np.arange(GRID_W * GRID_W)] = mask.reshape(-1)
    t = jnp.einsum('hyx,xq->hyq', rpb.astype(F32), jnp.asarray(onehot), precision=lax.Precision.HIGHEST)
    t = t + jnp.asarray(np.where(mask.reshape(-1), 0.0, NEG).astype(np.float32))
    t = jnp.stack([t[:, o:o + WIN_H] for o in range(WIN_H)], axis=1)
    t = t.reshape(NA_HEADS, WIN_H, WIN_H, GRID_W, GRID_W)
    t = t.transpose(0, 1, 3, 2, 4).reshape(NA_HEADS, WIN_H, GRID_W, WIN_H * GRID_W)
    t = t.reshape(NA_HEADS // 2, 2, WIN_H, GRID_W, WIN_H * GRID_W).transpose(0, 2, 1, 3, 4)
    return t.reshape(NA_HEADS // 2, WIN_H, 2 * GRID_W, WIN_H * GRID_W)


def _route(x, nw_ref, whi_ref, wlo_ref, ids_ref, wts_ref):
    ms = jnp.mean(x * x, axis=-1, keepdims=True)
    h = x * lax.rsqrt(ms + EPS) * nw_ref[...]
    hh = h.astype(BF16)
    hl = (h - hh.astype(F32)).astype(BF16)
    logits = _dot(hh, whi_ref[...]) + (_dot(hl, whi_ref[...]) + _dot(hh, wlo_ref[...]))
    col = lax.broadcasted_iota(jnp.int32, logits.shape, 1).astype(F32)
    logits = jnp.where(col < N_EXPERTS, logits, -jnp.inf)
    m1 = jnp.max(logits, axis=-1, keepdims=True)
    i1 = jnp.min(jnp.where(logits == m1, col, float(LANES)), axis=-1, keepdims=True)
    rest = jnp.where(col == i1, -jnp.inf, logits)
    m2 = jnp.max(rest, axis=-1, keepdims=True)
    i2 = jnp.min(jnp.where(rest == m2, col, float(LANES)), axis=-1, keepdims=True)
    e = jnp.exp(m2 - m1)
    w1 = 1.0 / (1.0 + e)
    w2 = e / (1.0 + e)
    ids_ref[...] = jnp.where(col == 0.0, i1, jnp.where(col == 1.0, i2, 0.0)).astype(jnp.int32)
    wts_ref[...] = jnp.where(col == 0.0, w1, jnp.where(col == 1.0, w2, 0.0))


def _outproj_body(of_ref, ob_ref, g_ref, ona_ref, x_ref, gw_ref, w_ref, *rest):
    xo_ref = rest[-1] if len(rest) == 1 else rest[3]
    o = of_ref[...] + ob_ref[...]
    g = g_ref[...]
    gate = g / (1.0 + jnp.exp(-g))
    parts = []
    for h in range(HG_HEADS):
        sl = slice(h * HG_D, (h + 1) * HG_D)
        oh = o[:, sl]
        ms = jnp.mean(oh * oh, axis=-1, keepdims=True)
        parts.append(oh * lax.rsqrt(ms + EPS) * gw_ref[...] * gate[:, sl])
    hg = jnp.concatenate(parts, axis=1).astype(BF16)
    y = _dot(hg, w_ref[0:HG_W, :]) + _dot(ona_ref[...], w_ref[HG_W:, :])
    x = x_ref[...] + y
    xo_ref[...] = x
    if len(rest) > 1:
        nw_ref, whi_ref, wlo_ref, _, ids_ref, wts_ref = rest
        _route(x, nw_ref, whi_ref, wlo_ref, ids_ref, wts_ref)


def _outproj(o_f, o_b, uhg, o_na, x, gw, w, route=None):
    n, d = x.shape
    tok = lambda width, col=0: pl.BlockSpec((TM, width), lambda i: (i, col))
    in_specs = [tok(HG_W), tok(HG_W), tok(HG_W, 4), tok(NA_W), tok(d), _resident(gw.shape), _resident(w.shape)]
    out_specs = [tok(d)]
    out_shape = [jax.ShapeDtypeStruct((n, d), F32)]
    args = [o_f, o_b, uhg, o_na, x, gw, w]
    if route is not None:
        in_specs += [_resident(a.shape) for a in route]
        args += list(route)
        out_specs += [tok(LANES), tok(LANES)]
        out_shape += [jax.ShapeDtypeStruct((n, LANES), jnp.int32), jax.ShapeDtypeStruct((n, LANES), F32)]
    out = pl.pallas_call(
        _outproj_body,
        grid=(n // TM,),
        in_specs=in_specs,
        out_specs=out_specs,
        out_shape=out_shape,
        compiler_params=_params("arbitrary"),
        name="outproj_route" if route is not None else "outproj",
    )(*args)
    return out[0] if route is None else out


def _ffn_body(te_ref, nv_ref, x_ref, nw_ref, wg_ref, wu_ref, wd_ref, o_ref, xn_ref, *, residual):
    del te_ref
    i = pl.program_id(0)
    f = pl.program_id(1)
    valid = i < nv_ref[0]

    @pl.when(jnp.logical_and(f == 0, valid))
    def _():
        xr = x_ref[...]
        ms = jnp.mean(xr * xr, axis=-1, keepdims=True)
        xn_ref[...] = (xr * lax.rsqrt(ms + EPS) * nw_ref[...]).astype(BF16)
        o_ref[...] = xr if residual else jnp.zeros_like(xr)

    @pl.when(jnp.logical_and(f == 0, jnp.logical_not(valid)))
    def _():
        o_ref[...] = jnp.zeros_like(o_ref)

    @pl.when(valid)
    def _():
        x = xn_ref[...]
        tf = wg_ref.shape[2]
        for c0 in range(0, tf, FFN_SUB):
            c1 = min(c0 + FFN_SUB, tf)
            gate = _dot(x, wg_ref[0, :, c0:c1])
            up = _dot(x, wu_ref[0, :, c0:c1])
            hid = (gate / (1.0 + jnp.exp(-gate)) * up).astype(BF16)
            o_ref[...] += _dot(hid, wd_ref[0, c0:c1, :])


def _ffn(x, nw, wg, wu, wd, tile_expert, n_valid, tm, tf, residual):
    p, d = x.shape
    ff = wg.shape[2]
    assert ff % tf == 0 and tf % MXU_N == 0
    once = dict(pipeline_mode=pl.Buffered(1)) if (wg.shape[0] == 1 and tf == ff) else {}
    row = pl.BlockSpec((tm, d), lambda i, f, te, nv: (i, 0))
    grid_spec = pltpu.PrefetchScalarGridSpec(
        num_scalar_prefetch=2,
        grid=(p // tm, ff // tf),
        in_specs=[row,
                  pl.BlockSpec(nw.shape, lambda i, f, te, nv: (0, 0)),
                  pl.BlockSpec((1, d, tf), lambda i, f, te, nv: (te[i], 0, f), **once),
                  pl.BlockSpec((1, d, tf), lambda i, f, te, nv: (te[i], 0, f), **once),
                  pl.BlockSpec((1, tf, d), lambda i, f, te, nv: (te[i], f, 0), **once)],
        out_specs=row,
        scratch_shapes=[pltpu.VMEM((tm, d), BF16)],
    )
    return pl.pallas_call(
        functools.partial(_ffn_body, residual=residual),
        grid_spec=grid_spec,
        out_shape=jax.ShapeDtypeStruct((p, d), F32),
        compiler_params=pltpu.CompilerParams(dimension_semantics=("arbitrary", "arbitrary"),
                                             vmem_limit_bytes=FFN_VMEM_LIMIT),
        name="ffn_res" if residual else "ffn_grouped",
    )(tile_expert, n_valid, x, nw, wg, wu, wd)


def _row_copy(src_ref, dst_ref, sem, src_row, dst_row):
    return pltpu.make_async_copy(src_ref.at[pl.ds(src_row, 1)], dst_ref.at[pl.ds(dst_row, 1)], sem)


def _gather_into(idx_smem, src_hbm, dst_ref, sem):
    def start(b, carry):
        for u in range(DMA_UNROLL):
            r = b * DMA_UNROLL + u
            _row_copy(src_hbm, dst_ref, sem, idx_smem[r], r).start(priority=u % 2)
        return carry

    lax.fori_loop(0, GATHER_ROWS // DMA_UNROLL, start, 0)

    def wait(b, carry):
        for u in range(DMA_UNROLL):
            _row_copy(src_hbm, dst_ref, sem, 0, b * DMA_UNROLL + u).wait()
        return carry

    lax.fori_loop(0, GATHER_ROWS // DMA_UNROLL, wait, 0)


def _gather_body(idx_hbm, src_hbm, o_ref, idx_smem, sem_idx, sem_rows):
    i = pl.program_id(0)
    cp = pltpu.make_async_copy(idx_hbm.at[pl.ds(i * GATHER_ROWS, GATHER_ROWS)], idx_smem, sem_idx)
    cp.start()
    cp.wait()
    _gather_into(idx_smem, src_hbm, o_ref, sem_rows)


def _gather_rows(idx, src):
    p = idx.shape[0]
    w = src.shape[1]
    return pl.pallas_call(
        _gather_body,
        grid=(p // GATHER_ROWS,),
        in_specs=[pl.BlockSpec(memory_space=pl.ANY), pl.BlockSpec(memory_space=pl.ANY)],
        out_specs=pl.BlockSpec((GATHER_ROWS, w), lambda i: (i, 0)),
        out_shape=jax.ShapeDtypeStruct((p, w), src.dtype),
        scratch_shapes=[pltpu.SMEM((GATHER_ROWS,), jnp.int32),
                        pltpu.SemaphoreType.DMA(()), pltpu.SemaphoreType.DMA(())],
        compiler_params=_params("arbitrary"),
        name="gather_rows",
    )(idx, src)


def _combine_body(pos_hbm, y_hbm, x_ref, wts_ref, o_ref, pos_smem, buf_ref, sem_idx, sem_rows):
    i = pl.program_id(0)
    tc = GATHER_ROWS // TOP_K
    cp = pltpu.make_async_copy(pos_hbm.at[pl.ds(i * GATHER_ROWS, GATHER_ROWS)], pos_smem, sem_idx)
    cp.start()
    cp.wait()
    _gather_into(pos_smem, y_hbm, buf_ref, sem_rows)
    w = wts_ref[...]
    o_ref[...] = x_ref[...] + (w[:, 0:1] * buf_ref[0:tc, :] + w[:, 1:2] * buf_ref[tc:2 * tc, :])


def _combine(pos, y, x, wts):
    n, d = x.shape
    tc = GATHER_ROWS // TOP_K
    return pl.pallas_call(
        _combine_body,
        grid=(n // tc,),
        in_specs=[pl.BlockSpec(memory_space=pl.ANY), pl.BlockSpec(memory_space=pl.ANY),
                  pl.BlockSpec((tc, d), lambda i: (i, 0)), pl.BlockSpec((tc, LANES), lambda i: (i, 0))],
        out_specs=pl.BlockSpec((tc, d), lambda i: (i, 0)),
        out_shape=jax.ShapeDtypeStruct((n, d), F32),
        scratch_shapes=[pltpu.SMEM((GATHER_ROWS,), jnp.int32), pltpu.VMEM((GATHER_ROWS, d), F32),
                        pltpu.SemaphoreType.DMA(()), pltpu.SemaphoreType.DMA(())],
        compiler_params=_params("arbitrary"),
        name="combine",
    )(pos, y, x, wts)


def _moe(x, ids, wts, nw, wg, wu, wd):
    n, d = x.shape
    e_flat = ids[:, :TOP_K].reshape(-1)
    na = n * TOP_K
    onehot = (e_flat[:, None] == jnp.arange(N_EXPERTS, dtype=jnp.int32)[None, :]).astype(jnp.int32)
    csum = jnp.cumsum(onehot, axis=0)
    counts = csum[-1]
    rank = jnp.sum(csum * onehot, axis=1) - 1
    padded = (counts + FFN_TM - 1) // FFN_TM * FFN_TM
    pad_end = jnp.cumsum(padded)
    pad_off = pad_end - padded
    raw_off = jnp.cumsum(counts) - counts
    pos = jnp.sum(onehot * pad_off[None, :], axis=1) + rank
    n_rows = (na + N_EXPERTS * (FFN_TM - 1)) // FFN_TM * FFN_TM
    tile_start = jnp.arange(n_rows // FFN_TM, dtype=jnp.int32) * FFN_TM
    tile_expert = jnp.minimum(
        jnp.sum(tile_start[:, None] >= pad_end[None, :], axis=1), N_EXPERTS - 1).astype(jnp.int32)
    n_valid = (pad_end[-1] // FFN_TM).astype(jnp.int32).reshape(1)
    order = jnp.argsort(e_flat, stable=True).astype(jnp.int32)
    row = jnp.arange(n_rows, dtype=jnp.int32)
    e_row = jnp.repeat(tile_expert, FFN_TM)
    local = row - pad_off[e_row]
    src = jnp.clip(raw_off[e_row] + local, 0, na - 1)
    src_tok = jnp.where(local < counts[e_row], order[src] // TOP_K, 0).astype(jnp.int32)

    xs = _gather_rows(src_tok, x)
    y = _ffn(xs, nw, wg, wu, wd, tile_expert, n_valid, tm=FFN_TM, tf=wg.shape[2] // 2, residual=False)
    tc = GATHER_ROWS // TOP_K
    pos_tiles = pos.astype(jnp.int32).reshape(n // tc, tc, TOP_K).transpose(0, 2, 1).reshape(-1)
    return _combine(pos_tiles, y, x, wts)


def kernel(x_prompt, x_sample, norm_mix, w_in, lb_logits, hg_gnorm, na_qnorm, na_knorm, na_rpb, w_out, norm_ffn,
           ffn_gate, ffn_up, ffn_down, router, exp_gate, exp_up, exp_down):
    bp, tp, d = x_prompt.shape
    bs, ts, _ = x_sample.shape
    depth = w_in.shape[0]
    seq_lens = [tp] * bp + [ts] * bs
    assert tp % HG_TB == 0 and ts % HG_TB == 0 and min(tp, ts) // GRID_W >= WIN_H
    n = bp * tp + bs * ts
    assert n % FFN_TM == 0 and d % 2 == 0
    x = jnp.concatenate([x_prompt.reshape(bp * tp, d), x_sample.reshape(bs * ts, d)], axis=0)

    lb = jnp.cumsum(jax.nn.softmax(lb_logits.astype(F32), axis=1), axis=1)
    lb = lb - lb[:, :1]
    lbp = jnp.stack([jnp.log(lb), jnp.log1p(-lb)], axis=2)

    seg = jnp.asarray(np.kron(np.eye(MXU_N // NA_DH), np.ones((NA_DH, NA_DH))) / NA_DH, BF16)
    scale = NA_DH ** -0.5

    for l in range(depth):
        qkw = jnp.stack([jnp.tile(na_qnorm[l].astype(F32), NA_HEADS) * scale,
                         jnp.tile(na_knorm[l].astype(F32), NA_HEADS)])
        uhg, una = _inproj(x, norm_mix[l].reshape(1, d), w_in[l].astype(BF16), qkw, seg)
        o_f, o_b = _hgrn(uhg, lbp[0, l], lbp[1, l], seq_lens)
        o_na = _natten(una, _na_bias_table(na_rpb[l]), seq_lens)
        nw = norm_ffn[l].reshape(1, d)
        j = l // 2
        gw = hg_gnorm[l].reshape(1, HG_D)
        if l % 2 == 0:
            x = _outproj(o_f, o_b, uhg, o_na, x, gw, w_out[l].astype(BF16))
            n_tiles = n // FFN_TM
            x = _ffn(x, nw, ffn_gate[j:j + 1].astype(BF16), ffn_up[j:j + 1].astype(BF16),
                     ffn_down[j:j + 1].astype(BF16), jnp.zeros((n_tiles,), jnp.int32),
                     jnp.full((1,), n_tiles, jnp.int32), tm=FFN_TM, tf=ffn_gate.shape[2], residual=True)
        else:
            wr = jnp.pad(router[j].astype(F32), ((0, 0), (0, LANES - N_EXPERTS)))
            whi = wr.astype(BF16)
            wlo = (wr - whi.astype(F32)).astype(BF16)
            x, ids, wts = _outproj(o_f, o_b, uhg, o_na, x, gw, w_out[l].astype(BF16), route=(nw, whi, wlo))
            x = _moe(x, ids, wts, nw, exp_gate[j].astype(BF16), exp_up[j].astype(BF16), exp_down[j].astype(BF16))

    y_prompt = x[:bp * tp].reshape(bp, tp, d)
    y_sample = x[bp * tp:].reshape(bs, ts, d)
    return (y_prompt, y_sample)
```

```python
import functools

import numpy as np
import jax
import jax.numpy as jnp
from jax import lax
from jax.experimental import pallas as pl
from jax.experimental.pallas import tpu as pltpu

F32 = jnp.float32
BF16 = jnp.bfloat16

GRID_W = 64
HG_HEADS, HG_D = 4, 128
HG_W = HG_HEADS * HG_D
NA_HEADS, NA_DH = 8, 64
NA_W = NA_HEADS * NA_DH
WIN_H, WIN_W = 8, 16
N_EXPERTS, TOP_K = 8, 2
EPS = 1e-6
NEG = -1e30

LANES = 128
MXU_N = 256
DMA_UNROLL = 8
VMEM_LIMIT = 48 * 1024 * 1024
TM = 512
HG_TB = 512
HG_C = 128
HG_MAX_EXPONENT = 80.0
NA_RB_CHOICES = (32, 16, 8)
NA_QUAD = 4
NA_UNION = WIN_H + NA_QUAD
FFN_TM = 1024
FFN_SUB = 2 * MXU_N
FFN_VMEM_LIMIT = 54 * 1024 * 1024
GATHER_ROWS = 1024


def _dot(a, b):
    return jnp.dot(a, b, preferred_element_type=F32)


def _dot_nt(a, b):
    return lax.dot_general(a, b, (((1,), (1,)), ((), ())), preferred_element_type=F32)


def _dot_tn(a, b):
    return lax.dot_general(a, b, (((0,), (0,)), ((), ())), preferred_element_type=F32)


def _params(*sem):
    return pltpu.CompilerParams(dimension_semantics=sem, vmem_limit_bytes=VMEM_LIMIT)


def _resident(shape):
    nd = len(shape)
    return pl.BlockSpec(shape, lambda *_: (0,) * nd, pipeline_mode=pl.Buffered(1))


def _inproj_body(x_ref, nw_ref, w_ref, qkw_ref, seg_ref, uhg_ref, una_ref):
    x = x_ref[...]
    ms = jnp.mean(x * x, axis=-1, keepdims=True)
    h = (x * lax.rsqrt(ms + EPS) * nw_ref[...]).astype(BF16)
    n_hg = uhg_ref.shape[1]
    uhg_ref[...] = _dot(h, w_ref[:, 0:n_hg])
    una_ref[:, 2 * NA_W:] = _dot(h, w_ref[:, n_hg + 2 * NA_W:]).astype(BF16)
    seg = seg_ref[...]
    cw = seg.shape[0]
    cols = [(part, c) for part in range(2) for c in range(0, NA_W, cw)]
    ys = [_dot(h, w_ref[:, n_hg + part * NA_W + c:n_hg + part * NA_W + c + cw]) for part, c in cols]
    for (part, c), y in zip(cols, ys):
        msq = _dot((y * y).astype(BF16), seg)
        y = y * lax.rsqrt(msq + EPS) * qkw_ref[part:part + 1, c:c + cw]
        una_ref[:, part * NA_W + c:part * NA_W + c + cw] = y.astype(BF16)


def _inproj(x, nw, w, qkw, seg):
    n, d = x.shape
    n_hg = w.shape[1] - 3 * NA_W
    return pl.pallas_call(
        _inproj_body,
        grid=(n // TM,),
        in_specs=[
            pl.BlockSpec((TM, d), lambda i: (i, 0)),
            _resident(nw.shape), _resident(w.shape), _resident(qkw.shape), _resident(seg.shape),
        ],
        out_specs=[
            pl.BlockSpec((TM, n_hg), lambda i: (i, 0)),
            pl.BlockSpec((TM, 3 * NA_W), lambda i: (i, 0)),
        ],
        out_shape=[
            jax.ShapeDtypeStruct((n, n_hg), F32),
            jax.ShapeDtypeStruct((n, 3 * NA_W), BF16),
        ],
        compiler_params=_params("arbitrary"),
        name="inproj",
    )(x, nw, w, qkw, seg)


def _split2(x):
    hi = x.astype(BF16)
    lo = (x - hi.astype(F32)).astype(BF16)
    return hi, lo


def _hgrn_prep(q_ref, f_ref, lb_ref, tri_ref, qs_ref, ks_ref, bs_ref, r0, reverse):
    c = HG_C
    rows = pl.ds(r0, c)
    q = q_ref[rows, :]
    x = f_ref[rows, :]
    la = lb_ref[0:1, :]
    l1 = lb_ref[1:2, :]
    ls = jnp.minimum(x, 0.0) - jnp.log(1.0 + jnp.exp(-jnp.abs(x)))
    cc = l1 + ls
    g = jnp.maximum(la, cc) + jnp.log(1.0 + jnp.exp(-jnp.abs(la - cc)))
    bc = _dot(tri_ref[...], jnp.concatenate(_split2(g), axis=1))
    b = bc[:, :HG_W] + bc[:, HG_W:]
    qs_ref[rows, :] = q / (1.0 + jnp.exp(-q))
    ks_ref[rows, :] = jnp.exp(cc - x)
    bs_ref[rows, :] = b
    m = c // 2
    if reverse:
        near, total = b[m:m + 1, :], b[0:1, :]
    else:
        near, total = b[m - 1:m, :], b[c - 1:c, :]
    return jnp.maximum(-near, near - total)


def _hgrn_chunk(q_ref, k_ref, b_ref, v_ref, o_ref, st_ref, r0, reverse, fast):
    c = HG_C
    rows = pl.ds(r0, c)
    q = q_ref[rows, :]
    k = k_ref[rows, :]
    b = b_ref[rows, :]
    v = v_ref[rows, :]

    row = lax.broadcasted_iota(jnp.int32, (c, HG_D), 0)
    ri = lax.broadcasted_iota(jnp.int32, (c, c), 0)
    ci = lax.broadcasted_iota(jnp.int32, (c, c), 1)
    nb = c // 8
    i8 = lax.broadcasted_iota(jnp.int32, (nb, 8, HG_D), 1)

    for h in range(HG_HEADS):
        sl = slice(h * HG_D, (h + 1) * HG_D)
        bh, qh, kh, vh = b[:, sl], q[:, sl], k[:, sl], v[:, sl]
        vb = vh.astype(BF16)
        st = st_ref[h]
        btot = bh[0:1, :] if reverse else bh[c - 1:c, :]
        qe = (qh * jnp.exp(bh)).astype(BF16)
        ke = (kh * jnp.exp(btot - bh)).astype(BF16)
        o = _dot_nt(qe, st.astype(BF16))
        st_ref[h] = st * jnp.exp(btot) + _dot_tn(vb, ke)

        if fast:
            m = c // 2
            r = bh[m:m + 1, :] if reverse else bh[m - 1:m, :]
            qr = (qh * jnp.exp(bh - r)).astype(BF16)
            kr = (kh * jnp.exp(r - bh)).astype(BF16)
            causal = (ri <= ci) if reverse else (ri >= ci)
            a = jnp.where(causal, _dot_nt(qr, kr), 0.0)
            o_ref[rows, sl] = o + _dot(a.astype(BF16), vb)
            continue

        a = jnp.zeros((c, c), F32)
        w = c // 2
        while w >= 8:
            grp = c // (2 * w)
            b4 = bh.reshape(grp, 2 * w, HG_D)
            edge = w if reverse else w - 1
            r = jnp.broadcast_to(b4[:, edge:edge + 1, :], (grp, 2 * w, HG_D)).reshape(c, HG_D)
            e = jnp.exp(-jnp.abs(bh - r))
            late = (row & (2 * w - 1)) >= w
            is_q = jnp.logical_not(late) if reverse else late
            qs = jnp.where(is_q, qh * e, 0.0).astype(BF16)
            ks = jnp.where(is_q, 0.0, kh * e).astype(BF16)
            aw = _dot_nt(qs, ks)
            if grp > 1:
                sh = int(np.log2(2 * w))
                aw = jnp.where((ri >> sh) == (ci >> sh), aw, 0.0)
            a = a + aw
            w //= 2
        o = o + _dot(a.astype(BF16), vb)

        b3, q3, k3, v3 = (t.reshape(nb, 8, HG_D) for t in (bh, qh, kh, vh))
        od = jnp.zeros((nb, 8, HG_D), F32)
        for j in range(8):
            valid = (i8 <= j) if reverse else (i8 >= j)
            e = jnp.exp(jnp.where(valid, b3 - b3[:, j:j + 1, :], NEG))
            aj = jnp.sum(q3 * e * k3[:, j:j + 1, :], axis=-1, keepdims=True)
            od = od + aj * v3[:, j:j + 1, :]
        o_ref[rows, sl] = o + od.reshape(c, HG_D)


def _hgrn_body(fwd_ref, bwd_ref, first_ref, qf_ref, ff_ref, vf_ref, qb_ref, fb_ref, vb_ref,
               lbf_ref, lbb_ref, tril_ref, triu_ref, of_ref, ob_ref,
               stf_ref, stb_ref, qsf_ref, ksf_ref, bsf_ref, qsb_ref, ksb_ref, bsb_ref):
    del fwd_ref, bwd_ref
    g = pl.program_id(0)

    @pl.when(first_ref[g] == 1)
    def _():
        stf_ref[...] = jnp.zeros_like(stf_ref)
        stb_ref[...] = jnp.zeros_like(stb_ref)

    n_chunks = HG_TB // HG_C

    def prep(ci, worst):
        r0 = pl.multiple_of(ci * HG_C, HG_C)
        df = _hgrn_prep(qf_ref, ff_ref, lbf_ref, tril_ref, qsf_ref, ksf_ref, bsf_ref, r0, False)
        db = _hgrn_prep(qb_ref, fb_ref, lbb_ref, triu_ref, qsb_ref, ksb_ref, bsb_ref, r0, True)
        return jnp.maximum(worst, jnp.maximum(df, db))

    worst = jnp.max(lax.fori_loop(0, n_chunks, prep, jnp.zeros((1, HG_W), F32)))
    fits = worst <= HG_MAX_EXPONENT

    def run(fast):
        def body(ci, carry):
            rf = pl.multiple_of(ci * HG_C, HG_C)
            _hgrn_chunk(qsf_ref, ksf_ref, bsf_ref, vf_ref, of_ref, stf_ref, rf, False, fast)
            rb = pl.multiple_of((n_chunks - 1 - ci) * HG_C, HG_C)
            _hgrn_chunk(qsb_ref, ksb_ref, bsb_ref, vb_ref, ob_ref, stb_ref, rb, True, fast)
            return carry

        lax.fori_loop(0, n_chunks, body, 0, unroll=2 if fast else 1)

    @pl.when(fits)
    def _():
        run(True)

    @pl.when(jnp.logical_not(fits))
    def _():
        run(False)


def _hgrn(uhg, lbf, lbb, seq_lens):
    n = uhg.shape[0]
    fwd, bwd, first = [], [], []
    base = 0
    for t in seq_lens:
        nt = t // HG_TB
        for j in range(nt):
            fwd.append(base + j)
            bwd.append(base + nt - 1 - j)
            first.append(1 if j == 0 else 0)
        base += nt
    fwd, bwd, first = (jnp.asarray(np.array(a, np.int32)) for a in (fwd, bwd, first))
    tril = jnp.asarray(np.tril(np.ones((HG_C, HG_C), np.float32)), BF16)
    triu = jnp.asarray(np.triu(np.ones((HG_C, HG_C), np.float32)), BF16)

    def spec(which, col):
        if which == 0:
            return pl.BlockSpec((HG_TB, HG_W), lambda g, f, b, s: (f[g], col))
        return pl.BlockSpec((HG_TB, HG_W), lambda g, f, b, s: (b[g], col))

    const = lambda shape: pl.BlockSpec(shape, lambda g, f, b, s: (0, 0))
    grid_spec = pltpu.PrefetchScalarGridSpec(
        num_scalar_prefetch=3,
        grid=(n // HG_TB,),
        in_specs=[spec(0, 0), spec(0, 1), spec(0, 3), spec(1, 0), spec(1, 2), spec(1, 3),
                  const(lbf.shape), const(lbb.shape), const(tril.shape), const(triu.shape)],
        out_specs=[spec(0, 0), spec(1, 0)],
        scratch_shapes=[pltpu.VMEM((HG_HEADS, HG_D, HG_D), F32)] * 2 + [pltpu.VMEM((HG_TB, HG_W), F32)] * 6,
    )
    return pl.pallas_call(
        _hgrn_body,
        grid_spec=grid_spec,
        out_shape=[jax.ShapeDtypeStruct((n, HG_W), F32), jax.ShapeDtypeStruct((n, HG_W), F32)],
        compiler_params=_params("arbitrary"),
        name="hgrn",
    )(fwd, bwd, first, uhg, uhg, uhg, uhg, uhg, uhg, lbf, lbb, tril, triu)


def _na_body(prev_ref, next_ref, rloc_ref, rtot_ref, q_ref, kp_ref, kc_ref, kn_ref,
             vp_ref, vc_ref, vn_ref, bias_ref, o_ref, ks_ref, vs_ref):
    del prev_ref, next_ref
    g = pl.program_id(1)
    blk = q_ref.shape[0]
    rb = blk // GRID_W
    ks_ref[0:blk, :] = kp_ref[...]
    ks_ref[blk:2 * blk, :] = kc_ref[...]
    ks_ref[2 * blk:3 * blk, :] = kn_ref[...]
    vs_ref[0:blk, :] = vp_ref[...]
    vs_ref[blk:2 * blk, :] = vc_ref[...]
    vs_ref[2 * blk:3 * blk, :] = vn_ref[...]
    r0 = rloc_ref[g]
    n_rows = rtot_ref[g]
    first_head = lax.broadcasted_iota(jnp.int32, (GRID_W, LANES), 1) < NA_DH
    nkeys = NA_UNION * GRID_W
    qrows = 2 * GRID_W

    for i0 in range(0, rb, NA_QUAD):
        rs0 = jnp.clip(r0 + i0 - WIN_H // 2, 0, n_rows - WIN_H)
        off = pl.multiple_of((rs0 - r0 + rb) * GRID_W, GRID_W)
        kw = ks_ref[pl.ds(off, nkeys), :]
        vw = vs_ref[pl.ds(off, nkeys), :]
        pieces = []
        for a in range(NA_QUAD):
            q2 = q_ref[(i0 + a) * GRID_W:(i0 + a + 1) * GRID_W, :]
            zero = jnp.zeros_like(q2)
            pieces += [jnp.where(first_head, q2, zero), jnp.where(first_head, zero, q2)]
        s = _dot_nt(jnp.concatenate(pieces, axis=0), kw)
        probs, sums = [], []
        for a in range(NA_QUAD):
            r = r0 + i0 + a
            rs = jnp.clip(r - WIN_H // 2, 0, n_rows - WIN_H)
            rel = rs - rs0
            entry = jnp.where(rel > 0, WIN_H - 1 + rel, rs - r + WIN_H - 1)
            sa = s[a * qrows:(a + 1) * qrows] + bias_ref[0, entry]
            p = jnp.exp(sa - jnp.max(sa, axis=-1, keepdims=True))
            sums.append(jnp.sum(p, axis=-1, keepdims=True))
            probs.append(p.astype(BF16))
        ov = _dot(jnp.concatenate(probs, axis=0), vw)
        for a in range(NA_QUAD):
            oa = ov[a * qrows:(a + 1) * qrows] / sums[a]
            out = jnp.where(first_head, oa[:GRID_W], oa[GRID_W:])
            o_ref[(i0 + a) * GRID_W:(i0 + a + 1) * GRID_W, :] = out.astype(o_ref.dtype)


def _natten(una, bias, seq_lens):
    n = una.shape[0]
    rb = max(c for c in NA_RB_CHOICES if all((t // GRID_W) % c == 0 for t in seq_lens))
    blk = rb * GRID_W
    prev, nxt, rloc, rtot = [], [], [], []
    base = 0
    for t in seq_lens:
        rows = t // GRID_W
        nb = rows // rb
        for j in range(nb):
            prev.append(base + max(j - 1, 0))
            nxt.append(base + min(j + 1, nb - 1))
            rloc.append(j * rb)
            rtot.append(rows)
        base += nb
    prev, nxt, rloc, rtot = (jnp.asarray(np.array(a, np.int32)) for a in (prev, nxt, rloc, rtot))
    npair = NA_W // LANES

    def cur(part):
        return pl.BlockSpec((blk, LANES), lambda p, g, pv, nx, rl, rt: (g, part * npair + p))

    def nbr(part, which):
        if which == 0:
            return pl.BlockSpec((blk, LANES), lambda p, g, pv, nx, rl, rt: (pv[g], part * npair + p))
        return pl.BlockSpec((blk, LANES), lambda p, g, pv, nx, rl, rt: (nx[g], part * npair + p))

    grid_spec = pltpu.PrefetchScalarGridSpec(
        num_scalar_prefetch=4,
        grid=(npair, n // blk),
        in_specs=[cur(0), nbr(1, 0), cur(1), nbr(1, 1), nbr(2, 0), cur(2), nbr(2, 1),
                  pl.BlockSpec((1,) + bias.shape[1:], lambda p, g, pv, nx, rl, rt: (p, 0, 0, 0))],
        out_specs=pl.BlockSpec((blk, LANES), lambda p, g, pv, nx, rl, rt: (g, p)),
        scratch_shapes=[pltpu.VMEM((3 * blk, LANES), BF16), pltpu.VMEM((3 * blk, LANES), BF16)],
    )
    return pl.pallas_call(
        _na_body,
        grid_spec=grid_spec,
        out_shape=jax.ShapeDtypeStruct((n, NA_W), BF16),
        compiler_params=_params("arbitrary", "arbitrary"),
        name="natten",
    )(prev, nxt, rloc, rtot, una, una, una, una, una, una, una, bias)


def _na_bias_table(rpb):
    c = np.arange(GRID_W)
    cstart = np.clip(c - WIN_W // 2, 0, GRID_W - WIN_W)
    kc = np.arange(GRID_W)
    mask = (kc[None, :] >= cstart[:, None]) & (kc[None, :] < cstart[:, None] + WIN_W)
    dx = np.clip(kc[None, :] - c[:, None] + WIN_W - 1, 0, 2 * WIN_W - 2)
    onehot = np.zeros((2 * WIN_W - 1, GRID_W * GRID_W), np.float32)
    onehot[dx.reshape(-1), np.arange(GRID_W * GRID_W)] = mask.reshape(-1)
    t = jnp.einsum('hyx,xq->hyq', rpb.astype(F32), jnp.asarray(onehot), precision=lax.Precision.HIGHEST)
    t = t + jnp.asarray(np.where(mask.reshape(-1), 0.0, NEG).astype(np.float32))
    def entry(o, rel):
        neg = lambda k: jnp.full((NA_HEADS, k, GRID_W * GRID_W), NEG, F32)
        return jnp.concatenate([neg(rel), t[:, o:o + WIN_H], neg(NA_UNION - WIN_H - rel)], axis=1)

    t = jnp.stack([entry(o, 0) for o in range(WIN_H)]
                  + [entry(WIN_H // 2 - 1, rel) for rel in range(1, NA_QUAD)], axis=1)
    ne = t.shape[1]
    t = t.reshape(NA_HEADS, ne, NA_UNION, GRID_W, GRID_W)
    t = t.transpose(0, 1, 3, 2, 4).reshape(NA_HEADS, ne, GRID_W, NA_UNION * GRID_W)
    t = t.reshape(NA_HEADS // 2, 2, ne, GRID_W, NA_UNION * GRID_W).transpose(0, 2, 1, 3, 4)
    return t.reshape(NA_HEADS // 2, ne, 2 * GRID_W, NA_UNION * GRID_W)


def _route(x, nw_ref, whi_ref, wlo_ref, ids_ref, wts_ref):
    ms = jnp.mean(x * x, axis=-1, keepdims=True)
    h = x * lax.rsqrt(ms + EPS) * nw_ref[...]
    hh = h.astype(BF16)
    hl = (h - hh.astype(F32)).astype(BF16)
    logits = _dot(hh, whi_ref[...]) + (_dot(hl, whi_ref[...]) + _dot(hh, wlo_ref[...]))
    col = lax.broadcasted_iota(jnp.int32, logits.shape, 1).astype(F32)
    logits = jnp.where(col < N_EXPERTS, logits, -jnp.inf)
    m1 = jnp.max(logits, axis=-1, keepdims=True)
    i1 = jnp.min(jnp.where(logits == m1, col, float(LANES)), axis=-1, keepdims=True)
    rest = jnp.where(col == i1, -jnp.inf, logits)
    m2 = jnp.max(rest, axis=-1, keepdims=True)
    i2 = jnp.min(jnp.where(rest == m2, col, float(LANES)), axis=-1, keepdims=True)
    e = jnp.exp(m2 - m1)
    w1 = 1.0 / (1.0 + e)
    w2 = e / (1.0 + e)
    ids_ref[...] = jnp.where(col == 0.0, i1, jnp.where(col == 1.0, i2, 0.0)).astype(jnp.int32)
    wts_ref[...] = jnp.where(col == 0.0, w1, jnp.where(col == 1.0, w2, 0.0))


def _outproj_body(of_ref, ob_ref, g_ref, ona_ref, x_ref, gw_ref, w_ref, *rest):
    xo_ref = rest[-1] if len(rest) == 1 else rest[3]
    o = of_ref[...] + ob_ref[...]
    g = g_ref[...]
    gate = g / (1.0 + jnp.exp(-g))
    parts = []
    for h in range(HG_HEADS):
        sl = slice(h * HG_D, (h + 1) * HG_D)
        oh = o[:, sl]
        ms = jnp.mean(oh * oh, axis=-1, keepdims=True)
        parts.append(oh * lax.rsqrt(ms + EPS) * gw_ref[...] * gate[:, sl])
    hg = jnp.concatenate(parts, axis=1).astype(BF16)
    y = _dot(hg, w_ref[0:HG_W, :]) + _dot(ona_ref[...], w_ref[HG_W:, :])
    x = x_ref[...] + y
    xo_ref[...] = x
    if len(rest) > 1:
        nw_ref, whi_ref, wlo_ref, _, ids_ref, wts_ref = rest
        _route(x, nw_ref, whi_ref, wlo_ref, ids_ref, wts_ref)


def _outproj(o_f, o_b, uhg, o_na, x, gw, w, route=None):
    n, d = x.shape
    tok = lambda width, col=0: pl.BlockSpec((TM, width), lambda i: (i, col))
    in_specs = [tok(HG_W), tok(HG_W), tok(HG_W, 4), tok(NA_W), tok(d), _resident(gw.shape), _resident(w.shape)]
    out_specs = [tok(d)]
    out_shape = [jax.ShapeDtypeStruct((n, d), F32)]
    args = [o_f, o_b, uhg, o_na, x, gw, w]
    if route is not None:
        in_specs += [_resident(a.shape) for a in route]
        args += list(route)
        out_specs += [tok(LANES), tok(LANES)]
        out_shape += [jax.ShapeDtypeStruct((n, LANES), jnp.int32), jax.ShapeDtypeStruct((n, LANES), F32)]
    out = pl.pallas_call(
        _outproj_body,
        grid=(n // TM,),
        in_specs=in_specs,
        out_specs=out_specs,
        out_shape=out_shape,
        compiler_params=_params("arbitrary"),
        name="outproj_route" if route is not None else "outproj",
    )(*args)
    return out[0] if route is None else out


def _ffn_body(te_ref, nv_ref, x_ref, nw_ref, wg_ref, wu_ref, wd_ref, o_ref, xn_ref, *, residual):
    del te_ref
    i = pl.program_id(0)
    f = pl.program_id(1)
    valid = i < nv_ref[0]

    @pl.when(jnp.logical_and(f == 0, valid))
    def _():
        xr = x_ref[...]
        ms = jnp.mean(xr * xr, axis=-1, keepdims=True)
        xn_ref[...] = (xr * lax.rsqrt(ms + EPS) * nw_ref[...]).astype(BF16)
        o_ref[...] = xr if residual else jnp.zeros_like(xr)

    @pl.when(jnp.logical_and(f == 0, jnp.logical_not(valid)))
    def _():
        o_ref[...] = jnp.zeros_like(o_ref)

    @pl.when(valid)
    def _():
        x = xn_ref[...]
        tf = wg_ref.shape[2]
        for c0 in range(0, tf, FFN_SUB):
            c1 = min(c0 + FFN_SUB, tf)
            gate = _dot(x, wg_ref[0, :, c0:c1])
            up = _dot(x, wu_ref[0, :, c0:c1])
            hid = (gate / (1.0 + jnp.exp(-gate)) * up).astype(BF16)
            o_ref[...] += _dot(hid, wd_ref[0, c0:c1, :])


def _ffn(x, nw, wg, wu, wd, tile_expert, n_valid, tm, tf, residual):
    p, d = x.shape
    ff = wg.shape[2]
    assert ff % tf == 0 and tf % MXU_N == 0
    once = dict(pipeline_mode=pl.Buffered(1)) if (wg.shape[0] == 1 and tf == ff) else {}
    row = pl.BlockSpec((tm, d), lambda i, f, te, nv: (i, 0))
    grid_spec = pltpu.PrefetchScalarGridSpec(
        num_scalar_prefetch=2,
        grid=(p // tm, ff // tf),
        in_specs=[row,
                  pl.BlockSpec(nw.shape, lambda i, f, te, nv: (0, 0)),
                  pl.BlockSpec((1, d, tf), lambda i, f, te, nv: (te[i], 0, f), **once),
                  pl.BlockSpec((1, d, tf), lambda i, f, te, nv: (te[i], 0, f), **once),
                  pl.BlockSpec((1, tf, d), lambda i, f, te, nv: (te[i], f, 0), **once)],
        out_specs=row,
        scratch_shapes=[pltpu.VMEM((tm, d), BF16)],
    )
    return pl.pallas_call(
        functools.partial(_ffn_body, residual=residual),
        grid_spec=grid_spec,
        out_shape=jax.ShapeDtypeStruct((p, d), F32),
        compiler_params=pltpu.CompilerParams(dimension_semantics=("arbitrary", "arbitrary"),
                                             vmem_limit_bytes=FFN_VMEM_LIMIT),
        name="ffn_res" if residual else "ffn_grouped",
    )(tile_expert, n_valid, x, nw, wg, wu, wd)


def _row_copy(src_ref, dst_ref, sem, src_row, dst_row):
    return pltpu.make_async_copy(src_ref.at[pl.ds(src_row, 1)], dst_ref.at[pl.ds(dst_row, 1)], sem)


def _gather_into(idx_smem, src_hbm, dst_ref, sem):
    def start(b, carry):
        for u in range(DMA_UNROLL):
            r = b * DMA_UNROLL + u
            _row_copy(src_hbm, dst_ref, sem, idx_smem[r], r).start(priority=u % 2)
        return carry

    lax.fori_loop(0, GATHER_ROWS // DMA_UNROLL, start, 0)

    def wait(b, carry):
        for u in range(DMA_UNROLL):
            _row_copy(src_hbm, dst_ref, sem, 0, b * DMA_UNROLL + u).wait()
        return carry

    lax.fori_loop(0, GATHER_ROWS // DMA_UNROLL, wait, 0)


def _gather_body(idx_hbm, src_hbm, o_ref, idx_smem, sem_idx, sem_rows):
    i = pl.program_id(0)
    cp = pltpu.make_async_copy(idx_hbm.at[pl.ds(i * GATHER_ROWS, GATHER_ROWS)], idx_smem, sem_idx)
    cp.start()
    cp.wait()
    _gather_into(idx_smem, src_hbm, o_ref, sem_rows)


def _gather_rows(idx, src):
    p = idx.shape[0]
    w = src.shape[1]
    return pl.pallas_call(
        _gather_body,
        grid=(p // GATHER_ROWS,),
        in_specs=[pl.BlockSpec(memory_space=pl.ANY), pl.BlockSpec(memory_space=pl.ANY)],
        out_specs=pl.BlockSpec((GATHER_ROWS, w), lambda i: (i, 0)),
        out_shape=jax.ShapeDtypeStruct((p, w), src.dtype),
        scratch_shapes=[pltpu.SMEM((GATHER_ROWS,), jnp.int32),
                        pltpu.SemaphoreType.DMA(()), pltpu.SemaphoreType.DMA(())],
        compiler_params=_params("arbitrary"),
        name="gather_rows",
    )(idx, src)


def _combine_body(pos_hbm, y_hbm, x_ref, wts_ref, o_ref, pos_smem, buf_ref, sem_idx, sem_rows):
    i = pl.program_id(0)
    tc = GATHER_ROWS // TOP_K
    cp = pltpu.make_async_copy(pos_hbm.at[pl.ds(i * GATHER_ROWS, GATHER_ROWS)], pos_smem, sem_idx)
    cp.start()
    cp.wait()
    _gather_into(pos_smem, y_hbm, buf_ref, sem_rows)
    w = wts_ref[...]
    o_ref[...] = x_ref[...] + (w[:, 0:1] * buf_ref[0:tc, :] + w[:, 1:2] * buf_ref[tc:2 * tc, :])


def _combine(pos, y, x, wts):
    n, d = x.shape
    tc = GATHER_ROWS // TOP_K
    return pl.pallas_call(
        _combine_body,
        grid=(n // tc,),
        in_specs=[pl.BlockSpec(memory_space=pl.ANY), pl.BlockSpec(memory_space=pl.ANY),
                  pl.BlockSpec((tc, d), lambda i: (i, 0)), pl.BlockSpec((tc, LANES), lambda i: (i, 0))],
        out_specs=pl.BlockSpec((tc, d), lambda i: (i, 0)),
        out_shape=jax.ShapeDtypeStruct((n, d), F32),
        scratch_shapes=[pltpu.SMEM((GATHER_ROWS,), jnp.int32), pltpu.VMEM((GATHER_ROWS, d), F32),
                        pltpu.SemaphoreType.DMA(()), pltpu.SemaphoreType.DMA(())],
        compiler_params=_params("arbitrary"),
        name="combine",
    )(pos, y, x, wts)


def _moe(x, ids, wts, nw, wg, wu, wd):
    n, d = x.shape
    e_flat = ids[:, :TOP_K].reshape(-1)
    na = n * TOP_K
    onehot = (e_flat[:, None] == jnp.arange(N_EXPERTS, dtype=jnp.int32)[None, :]).astype(jnp.int32)
    csum = jnp.cumsum(onehot, axis=0)
    counts = csum[-1]
    rank = jnp.sum(csum * onehot, axis=1) - 1
    padded = (counts + FFN_TM - 1) // FFN_TM * FFN_TM
    pad_end = jnp.cumsum(padded)
    pad_off = pad_end - padded
    raw_off = jnp.cumsum(counts) - counts
    pos = jnp.sum(onehot * pad_off[None, :], axis=1) + rank
    n_rows = (na + N_EXPERTS * (FFN_TM - 1)) // FFN_TM * FFN_TM
    tile_start = jnp.arange(n_rows // FFN_TM, dtype=jnp.int32) * FFN_TM
    tile_expert = jnp.minimum(
        jnp.sum(tile_start[:, None] >= pad_end[None, :], axis=1), N_EXPERTS - 1).astype(jnp.int32)
    n_valid = (pad_end[-1] // FFN_TM).astype(jnp.int32).reshape(1)
    order = jnp.argsort(e_flat, stable=True).astype(jnp.int32)
    row = jnp.arange(n_rows, dtype=jnp.int32)
    e_row = jnp.repeat(tile_expert, FFN_TM)
    local = row - pad_off[e_row]
    src = jnp.clip(raw_off[e_row] + local, 0, na - 1)
    src_tok = jnp.where(local < counts[e_row], order[src] // TOP_K, 0).astype(jnp.int32)

    xs = _gather_rows(src_tok, x)
    y = _ffn(xs, nw, wg, wu, wd, tile_expert, n_valid, tm=FFN_TM, tf=wg.shape[2] // 2, residual=False)
    tc = GATHER_ROWS // TOP_K
    pos_tiles = pos.astype(jnp.int32).reshape(n // tc, tc, TOP_K).transpose(0, 2, 1).reshape(-1)
    return _combine(pos_tiles, y, x, wts)


def kernel(x_prompt, x_sample, norm_mix, w_in, lb_logits, hg_gnorm, na_qnorm, na_knorm, na_rpb, w_out, norm_ffn,
           ffn_gate, ffn_up, ffn_down, router, exp_gate, exp_up, exp_down):
    bp, tp, d = x_prompt.shape
    bs, ts, _ = x_sample.shape
    depth = w_in.shape[0]
    seq_lens = [tp] * bp + [ts] * bs
    assert tp % HG_TB == 0 and ts % HG_TB == 0 and min(tp, ts) // GRID_W >= WIN_H
    n = bp * tp + bs * ts
    assert n % FFN_TM == 0 and d % 2 == 0
    x = jnp.concatenate([x_prompt.reshape(bp * tp, d), x_sample.reshape(bs * ts, d)], axis=0)

    lb = jnp.cumsum(jax.nn.softmax(lb_logits.astype(F32), axis=1), axis=1)
    lb = lb - lb[:, :1]
    lbp = jnp.stack([jnp.log(lb), jnp.log1p(-lb)], axis=2)

    seg = jnp.asarray(np.kron(np.eye(MXU_N // NA_DH), np.ones((NA_DH, NA_DH))) / NA_DH, BF16)
    scale = NA_DH ** -0.5

    for l in range(depth):
        qkw = jnp.stack([jnp.tile(na_qnorm[l].astype(F32), NA_HEADS) * scale,
                         jnp.tile(na_knorm[l].astype(F32), NA_HEADS)])
        uhg, una = _inproj(x, norm_mix[l].reshape(1, d), w_in[l].astype(BF16), qkw, seg)
        o_f, o_b = _hgrn(uhg, lbp[0, l], lbp[1, l], seq_lens)
        o_na = _natten(una, _na_bias_table(na_rpb[l]), seq_lens)
        nw = norm_ffn[l].reshape(1, d)
        j = l // 2
        gw = hg_gnorm[l].reshape(1, HG_D)
        if l % 2 == 0:
            x = _outproj(o_f, o_b, uhg, o_na, x, gw, w_out[l].astype(BF16))
            n_tiles = n // FFN_TM
            x = _ffn(x, nw, ffn_gate[j:j + 1].astype(BF16), ffn_up[j:j + 1].astype(BF16),
                     ffn_down[j:j + 1].astype(BF16), jnp.zeros((n_tiles,), jnp.int32),
                     jnp.full((1,), n_tiles, jnp.int32), tm=FFN_TM, tf=ffn_gate.shape[2], residual=True)
        else:
            wr = jnp.pad(router[j].astype(F32), ((0, 0), (0, LANES - N_EXPERTS)))
            whi = wr.astype(BF16)
            wlo = (wr - whi.astype(F32)).astype(BF16)
            x, ids, wts = _outproj(o_f, o_b, uhg, o_na, x, gw, w_out[l].astype(BF16), route=(nw, whi, wlo))
            x = _moe(x, ids, wts, nw, exp_gate[j].astype(BF16), exp_up[j].astype(BF16), exp_down[j].astype(BF16))

    y_prompt = x[:bp * tp].reshape(bp, tp, d)
    y_sample = x[bp * tp:].reshape(bs, ts, d)
    return (y_prompt, y_sample)
```

```python
import functools

import numpy as np
import jax
import jax.numpy as jnp
from jax import lax
from jax.experimental import pallas as pl
from jax.experimental.pallas import tpu as pltpu

F32 = jnp.float32
BF16 = jnp.bfloat16

GRID_W = 64
HG_HEADS, HG_D = 4, 128
HG_W = HG_HEADS * HG_D
NA_HEADS, NA_DH = 8, 64
NA_W = NA_HEADS * NA_DH
WIN_H, WIN_W = 8, 16
N_EXPERTS, TOP_K = 8, 2
EPS = 1e-6
NEG = -1e30

LANES = 128
MXU_N = 256
DMA_UNROLL = 8
VMEM_LIMIT = 48 * 1024 * 1024
TM = 512
HG_TB = 512
HG_C = 128
HG_MAX_EXPONENT = 80.0
NA_RB_CHOICES = (32, 16, 8)
NA_QUAD = 4
NA_UNION = WIN_H + NA_QUAD
FFN_TM = 1024
FFN_SUB = 2 * MXU_N
FFN_VMEM_LIMIT = 54 * 1024 * 1024
GATHER_ROWS = 2048


def _dot(a, b):
    return jnp.dot(a, b, preferred_element_type=F32)


def _dot_nt(a, b):
    return lax.dot_general(a, b, (((1,), (1,)), ((), ())), preferred_element_type=F32)


def _dot_tn(a, b):
    return lax.dot_general(a, b, (((0,), (0,)), ((), ())), preferred_element_type=F32)


def _params(*sem):
    return pltpu.CompilerParams(dimension_semantics=sem, vmem_limit_bytes=VMEM_LIMIT)


def _resident(shape):
    nd = len(shape)
    return pl.BlockSpec(shape, lambda *_: (0,) * nd, pipeline_mode=pl.Buffered(1))


def _inproj_body(x_ref, nw_ref, w_ref, qkw_ref, seg_ref, uhg_ref, una_ref):
    x = x_ref[...]
    ms = jnp.mean(x * x, axis=-1, keepdims=True)
    h = (x * lax.rsqrt(ms + EPS) * nw_ref[...]).astype(BF16)
    n_hg = uhg_ref.shape[1]
    uhg_ref[...] = _dot(h, w_ref[:, 0:n_hg])
    una_ref[:, 2 * NA_W:] = _dot(h, w_ref[:, n_hg + 2 * NA_W:]).astype(BF16)
    seg = seg_ref[...]
    cw = seg.shape[0]
    cols = [(part, c) for part in range(2) for c in range(0, NA_W, cw)]
    ys = [_dot(h, w_ref[:, n_hg + part * NA_W + c:n_hg + part * NA_W + c + cw]) for part, c in cols]
    for (part, c), y in zip(cols, ys):
        msq = _dot((y * y).astype(BF16), seg)
        y = y * lax.rsqrt(msq + EPS) * qkw_ref[part:part + 1, c:c + cw]
        una_ref[:, part * NA_W + c:part * NA_W + c + cw] = y.astype(BF16)


def _inproj(x, nw, w, qkw, seg):
    n, d = x.shape
    n_hg = w.shape[1] - 3 * NA_W
    return pl.pallas_call(
        _inproj_body,
        grid=(n // TM,),
        in_specs=[
            pl.BlockSpec((TM, d), lambda i: (i, 0)),
            _resident(nw.shape), _resident(w.shape), _resident(qkw.shape), _resident(seg.shape),
        ],
        out_specs=[
            pl.BlockSpec((TM, n_hg), lambda i: (i, 0)),
            pl.BlockSpec((TM, 3 * NA_W), lambda i: (i, 0)),
        ],
        out_shape=[
            jax.ShapeDtypeStruct((n, n_hg), F32),
            jax.ShapeDtypeStruct((n, 3 * NA_W), BF16),
        ],
        compiler_params=_params("arbitrary"),
        name="inproj",
    )(x, nw, w, qkw, seg)


def _split2(x):
    hi = x.astype(BF16)
    lo = (x - hi.astype(F32)).astype(BF16)
    return hi, lo


def _hgrn_prep(q_ref, f_ref, lb_ref, tri_ref, qs_ref, ks_ref, bs_ref, r0, reverse):
    c = HG_C
    rows = pl.ds(r0, c)
    q = q_ref[rows, :]
    x = f_ref[rows, :]
    la = lb_ref[0:1, :]
    l1 = lb_ref[1:2, :]
    ls = jnp.minimum(x, 0.0) - jnp.log(1.0 + jnp.exp(-jnp.abs(x)))
    cc = l1 + ls
    g = jnp.maximum(la, cc) + jnp.log(1.0 + jnp.exp(-jnp.abs(la - cc)))
    bc = _dot(tri_ref[...], jnp.concatenate(_split2(g), axis=1))
    b = bc[:, :HG_W] + bc[:, HG_W:]
    qs_ref[rows, :] = q / (1.0 + jnp.exp(-q))
    ks_ref[rows, :] = jnp.exp(cc - x)
    bs_ref[rows, :] = b
    m = c // 2
    if reverse:
        near, total = b[m:m + 1, :], b[0:1, :]
    else:
        near, total = b[m - 1:m, :], b[c - 1:c, :]
    return jnp.maximum(-near, near - total)


def _hgrn_chunk(q_ref, k_ref, b_ref, v_ref, o_ref, st_ref, r0, reverse, fast):
    c = HG_C
    rows = pl.ds(r0, c)
    q = q_ref[rows, :]
    k = k_ref[rows, :]
    b = b_ref[rows, :]
    v = v_ref[rows, :]

    row = lax.broadcasted_iota(jnp.int32, (c, HG_D), 0)
    ri = lax.broadcasted_iota(jnp.int32, (c, c), 0)
    ci = lax.broadcasted_iota(jnp.int32, (c, c), 1)
    nb = c // 8
    i8 = lax.broadcasted_iota(jnp.int32, (nb, 8, HG_D), 1)

    for h in range(HG_HEADS):
        sl = slice(h * HG_D, (h + 1) * HG_D)
        bh, qh, kh, vh = b[:, sl], q[:, sl], k[:, sl], v[:, sl]
        vb = vh.astype(BF16)
        st = st_ref[h]
        btot = bh[0:1, :] if reverse else bh[c - 1:c, :]
        qe = (qh * jnp.exp(bh)).astype(BF16)
        ke = (kh * jnp.exp(btot - bh)).astype(BF16)
        o = _dot_nt(qe, st.astype(BF16))
        st_ref[h] = st * jnp.exp(btot) + _dot_tn(vb, ke)

        if fast:
            m = c // 2
            r = bh[m:m + 1, :] if reverse else bh[m - 1:m, :]
            qr = (qh * jnp.exp(bh - r)).astype(BF16)
            kr = (kh * jnp.exp(r - bh)).astype(BF16)
            causal = (ri <= ci) if reverse else (ri >= ci)
            a = jnp.where(causal, _dot_nt(qr, kr), 0.0)
            o_ref[rows, sl] = o + _dot(a.astype(BF16), vb)
            continue

        a = jnp.zeros((c, c), F32)
        w = c // 2
        while w >= 8:
            grp = c // (2 * w)
            b4 = bh.reshape(grp, 2 * w, HG_D)
            edge = w if reverse else w - 1
            r = jnp.broadcast_to(b4[:, edge:edge + 1, :], (grp, 2 * w, HG_D)).reshape(c, HG_D)
            e = jnp.exp(-jnp.abs(bh - r))
            late = (row & (2 * w - 1)) >= w
            is_q = jnp.logical_not(late) if reverse else late
            qs = jnp.where(is_q, qh * e, 0.0).astype(BF16)
            ks = jnp.where(is_q, 0.0, kh * e).astype(BF16)
            aw = _dot_nt(qs, ks)
            if grp > 1:
                sh = int(np.log2(2 * w))
                aw = jnp.where((ri >> sh) == (ci >> sh), aw, 0.0)
            a = a + aw
            w //= 2
        o = o + _dot(a.astype(BF16), vb)

        b3, q3, k3, v3 = (t.reshape(nb, 8, HG_D) for t in (bh, qh, kh, vh))
        od = jnp.zeros((nb, 8, HG_D), F32)
        for j in range(8):
            valid = (i8 <= j) if reverse else (i8 >= j)
            e = jnp.exp(jnp.where(valid, b3 - b3[:, j:j + 1, :], NEG))
            aj = jnp.sum(q3 * e * k3[:, j:j + 1, :], axis=-1, keepdims=True)
            od = od + aj * v3[:, j:j + 1, :]
        o_ref[rows, sl] = o + od.reshape(c, HG_D)


def _hgrn_body(fwd_ref, bwd_ref, first_ref, qf_ref, ff_ref, vf_ref, qb_ref, fb_ref, vb_ref,
               lbf_ref, lbb_ref, tril_ref, triu_ref, of_ref, ob_ref,
               stf_ref, stb_ref, qsf_ref, ksf_ref, bsf_ref, qsb_ref, ksb_ref, bsb_ref):
    del fwd_ref, bwd_ref
    g = pl.program_id(0)

    @pl.when(first_ref[g] == 1)
    def _():
        stf_ref[...] = jnp.zeros_like(stf_ref)
        stb_ref[...] = jnp.zeros_like(stb_ref)

    n_chunks = HG_TB // HG_C

    def prep(ci, worst):
        r0 = pl.multiple_of(ci * HG_C, HG_C)
        df = _hgrn_prep(qf_ref, ff_ref, lbf_ref, tril_ref, qsf_ref, ksf_ref, bsf_ref, r0, False)
        db = _hgrn_prep(qb_ref, fb_ref, lbb_ref, triu_ref, qsb_ref, ksb_ref, bsb_ref, r0, True)
        return jnp.maximum(worst, jnp.maximum(df, db))

    worst = jnp.max(lax.fori_loop(0, n_chunks, prep, jnp.zeros((1, HG_W), F32)))
    fits = worst <= HG_MAX_EXPONENT

    def run(fast):
        def body(ci, carry):
            rf = pl.multiple_of(ci * HG_C, HG_C)
            _hgrn_chunk(qsf_ref, ksf_ref, bsf_ref, vf_ref, of_ref, stf_ref, rf, False, fast)
            rb = pl.multiple_of((n_chunks - 1 - ci) * HG_C, HG_C)
            _hgrn_chunk(qsb_ref, ksb_ref, bsb_ref, vb_ref, ob_ref, stb_ref, rb, True, fast)
            return carry

        lax.fori_loop(0, n_chunks, body, 0, unroll=2 if fast else 1)

    @pl.when(fits)
    def _():
        run(True)

    @pl.when(jnp.logical_not(fits))
    def _():
        run(False)


def _hgrn(uhg, lbf, lbb, seq_lens):
    n = uhg.shape[0]
    fwd, bwd, first = [], [], []
    base = 0
    for t in seq_lens:
        nt = t // HG_TB
        for j in range(nt):
            fwd.append(base + j)
            bwd.append(base + nt - 1 - j)
            first.append(1 if j == 0 else 0)
        base += nt
    fwd, bwd, first = (jnp.asarray(np.array(a, np.int32)) for a in (fwd, bwd, first))
    tril = jnp.asarray(np.tril(np.ones((HG_C, HG_C), np.float32)), BF16)
    triu = jnp.asarray(np.triu(np.ones((HG_C, HG_C), np.float32)), BF16)

    def spec(which, col):
        if which == 0:
            return pl.BlockSpec((HG_TB, HG_W), lambda g, f, b, s: (f[g], col))
        return pl.BlockSpec((HG_TB, HG_W), lambda g, f, b, s: (b[g], col))

    const = lambda shape: pl.BlockSpec(shape, lambda g, f, b, s: (0, 0))
    grid_spec = pltpu.PrefetchScalarGridSpec(
        num_scalar_prefetch=3,
        grid=(n // HG_TB,),
        in_specs=[spec(0, 0), spec(0, 1), spec(0, 3), spec(1, 0), spec(1, 2), spec(1, 3),
                  const(lbf.shape), const(lbb.shape), const(tril.shape), const(triu.shape)],
        out_specs=[spec(0, 0), spec(1, 0)],
        scratch_shapes=[pltpu.VMEM((HG_HEADS, HG_D, HG_D), F32)] * 2 + [pltpu.VMEM((HG_TB, HG_W), F32)] * 6,
    )
    return pl.pallas_call(
        _hgrn_body,
        grid_spec=grid_spec,
        out_shape=[jax.ShapeDtypeStruct((n, HG_W), F32), jax.ShapeDtypeStruct((n, HG_W), F32)],
        compiler_params=_params("arbitrary"),
        name="hgrn",
    )(fwd, bwd, first, uhg, uhg, uhg, uhg, uhg, uhg, lbf, lbb, tril, triu)


def _na_body(prev_ref, next_ref, rloc_ref, rtot_ref, q_ref, kp_ref, kc_ref, kn_ref,
             vp_ref, vc_ref, vn_ref, bias_ref, o_ref, ks_ref, vs_ref):
    del prev_ref, next_ref
    g = pl.program_id(1)
    blk = q_ref.shape[0]
    rb = blk // GRID_W
    ks_ref[0:blk, :] = kp_ref[...]
    ks_ref[blk:2 * blk, :] = kc_ref[...]
    ks_ref[2 * blk:3 * blk, :] = kn_ref[...]
    vs_ref[0:blk, :] = vp_ref[...]
    vs_ref[blk:2 * blk, :] = vc_ref[...]
    vs_ref[2 * blk:3 * blk, :] = vn_ref[...]
    r0 = rloc_ref[g]
    n_rows = rtot_ref[g]
    first_head = lax.broadcasted_iota(jnp.int32, (GRID_W, LANES), 1) < NA_DH
    nkeys = NA_UNION * GRID_W
    qrows = 2 * GRID_W

    for i0 in range(0, rb, NA_QUAD):
        rs0 = jnp.clip(r0 + i0 - WIN_H // 2, 0, n_rows - WIN_H)
        off = pl.multiple_of((rs0 - r0 + rb) * GRID_W, GRID_W)
        kw = ks_ref[pl.ds(off, nkeys), :]
        vw = vs_ref[pl.ds(off, nkeys), :]
        pieces = []
        for a in range(NA_QUAD):
            q2 = q_ref[(i0 + a) * GRID_W:(i0 + a + 1) * GRID_W, :]
            zero = jnp.zeros_like(q2)
            pieces += [jnp.where(first_head, q2, zero), jnp.where(first_head, zero, q2)]
        s = _dot_nt(jnp.concatenate(pieces, axis=0), kw)
        probs, sums = [], []
        for a in range(NA_QUAD):
            r = r0 + i0 + a
            rs = jnp.clip(r - WIN_H // 2, 0, n_rows - WIN_H)
            rel = rs - rs0
            entry = jnp.where(rel > 0, WIN_H - 1 + rel, rs - r + WIN_H - 1)
            sa = s[a * qrows:(a + 1) * qrows] + bias_ref[0, entry]
            p = jnp.exp(sa - jnp.max(sa, axis=-1, keepdims=True))
            sums.append(jnp.sum(p, axis=-1, keepdims=True))
            probs.append(p.astype(BF16))
        ov = _dot(jnp.concatenate(probs, axis=0), vw)
        for a in range(NA_QUAD):
            oa = ov[a * qrows:(a + 1) * qrows] / sums[a]
            out = jnp.where(first_head, oa[:GRID_W], oa[GRID_W:])
            o_ref[(i0 + a) * GRID_W:(i0 + a + 1) * GRID_W, :] = out.astype(o_ref.dtype)


def _natten(una, bias, seq_lens):
    n = una.shape[0]
    rb = max(c for c in NA_RB_CHOICES if all((t // GRID_W) % c == 0 for t in seq_lens))
    blk = rb * GRID_W
    prev, nxt, rloc, rtot = [], [], [], []
    base = 0
    for t in seq_lens:
        rows = t // GRID_W
        nb = rows // rb
        for j in range(nb):
            prev.append(base + max(j - 1, 0))
            nxt.append(base + min(j + 1, nb - 1))
            rloc.append(j * rb)
            rtot.append(rows)
        base += nb
    prev, nxt, rloc, rtot = (jnp.asarray(np.array(a, np.int32)) for a in (prev, nxt, rloc, rtot))
    npair = NA_W // LANES

    def cur(part):
        return pl.BlockSpec((blk, LANES), lambda p, g, pv, nx, rl, rt: (g, part * npair + p))

    def nbr(part, which):
        if which == 0:
            return pl.BlockSpec((blk, LANES), lambda p, g, pv, nx, rl, rt: (pv[g], part * npair + p))
        return pl.BlockSpec((blk, LANES), lambda p, g, pv, nx, rl, rt: (nx[g], part * npair + p))

    grid_spec = pltpu.PrefetchScalarGridSpec(
        num_scalar_prefetch=4,
        grid=(npair, n // blk),
        in_specs=[cur(0), nbr(1, 0), cur(1), nbr(1, 1), nbr(2, 0), cur(2), nbr(2, 1),
                  pl.BlockSpec((1,) + bias.shape[1:], lambda p, g, pv, nx, rl, rt: (p, 0, 0, 0))],
        out_specs=pl.BlockSpec((blk, LANES), lambda p, g, pv, nx, rl, rt: (g, p)),
        scratch_shapes=[pltpu.VMEM((3 * blk, LANES), BF16), pltpu.VMEM((3 * blk, LANES), BF16)],
    )
    return pl.pallas_call(
        _na_body,
        grid_spec=grid_spec,
        out_shape=jax.ShapeDtypeStruct((n, NA_W), BF16),
        compiler_params=_params("arbitrary", "arbitrary"),
        name="natten",
    )(prev, nxt, rloc, rtot, una, una, una, una, una, una, una, bias)


def _na_bias_table(rpb):
    c = np.arange(GRID_W)
    cstart = np.clip(c - WIN_W // 2, 0, GRID_W - WIN_W)
    kc = np.arange(GRID_W)
    mask = (kc[None, :] >= cstart[:, None]) & (kc[None, :] < cstart[:, None] + WIN_W)
    dx = np.clip(kc[None, :] - c[:, None] + WIN_W - 1, 0, 2 * WIN_W - 2)
    onehot = np.zeros((2 * WIN_W - 1, GRID_W * GRID_W), np.float32)
    onehot[dx.reshape(-1), np.arange(GRID_W * GRID_W)] = mask.reshape(-1)
    t = jnp.einsum('hyx,xq->hyq', rpb.astype(F32), jnp.asarray(onehot), precision=lax.Precision.HIGHEST)
    t = t + jnp.asarray(np.where(mask.reshape(-1), 0.0, NEG).astype(np.float32))
    def entry(o, rel):
        neg = lambda k: jnp.full((NA_HEADS, k, GRID_W * GRID_W), NEG, F32)
        return jnp.concatenate([neg(rel), t[:, o:o + WIN_H], neg(NA_UNION - WIN_H - rel)], axis=1)

    t = jnp.stack([entry(o, 0) for o in range(WIN_H)]
                  + [entry(WIN_H // 2 - 1, rel) for rel in range(1, NA_QUAD)], axis=1)
    ne = t.shape[1]
    t = t.reshape(NA_HEADS, ne, NA_UNION, GRID_W, GRID_W)
    t = t.transpose(0, 1, 3, 2, 4).reshape(NA_HEADS, ne, GRID_W, NA_UNION * GRID_W)
    t = t.reshape(NA_HEADS // 2, 2, ne, GRID_W, NA_UNION * GRID_W).transpose(0, 2, 1, 3, 4)
    return t.reshape(NA_HEADS // 2, ne, 2 * GRID_W, NA_UNION * GRID_W)


def _route(x, nw_ref, whi_ref, wlo_ref, ids_ref, wts_ref):
    ms = jnp.mean(x * x, axis=-1, keepdims=True)
    h = x * lax.rsqrt(ms + EPS) * nw_ref[...]
    hh = h.astype(BF16)
    hl = (h - hh.astype(F32)).astype(BF16)
    logits = _dot(hh, whi_ref[...]) + (_dot(hl, whi_ref[...]) + _dot(hh, wlo_ref[...]))
    col = lax.broadcasted_iota(jnp.int32, logits.shape, 1).astype(F32)
    logits = jnp.where(col < N_EXPERTS, logits, -jnp.inf)
    m1 = jnp.max(logits, axis=-1, keepdims=True)
    i1 = jnp.min(jnp.where(logits == m1, col, float(LANES)), axis=-1, keepdims=True)
    rest = jnp.where(col == i1, -jnp.inf, logits)
    m2 = jnp.max(rest, axis=-1, keepdims=True)
    i2 = jnp.min(jnp.where(rest == m2, col, float(LANES)), axis=-1, keepdims=True)
    e = jnp.exp(m2 - m1)
    w1 = 1.0 / (1.0 + e)
    w2 = e / (1.0 + e)
    ids_ref[...] = jnp.where(col == 0.0, i1, jnp.where(col == 1.0, i2, 0.0)).astype(jnp.int32)
    wts_ref[...] = jnp.where(col == 0.0, w1, jnp.where(col == 1.0, w2, 0.0))


def _outproj_body(of_ref, ob_ref, g_ref, ona_ref, x_ref, gw_ref, w_ref, *rest):
    xo_ref = rest[-1] if len(rest) == 1 else rest[3]
    o = of_ref[...] + ob_ref[...]
    g = g_ref[...]
    gate = g / (1.0 + jnp.exp(-g))
    parts = []
    for h in range(HG_HEADS):
        sl = slice(h * HG_D, (h + 1) * HG_D)
        oh = o[:, sl]
        ms = jnp.mean(oh * oh, axis=-1, keepdims=True)
        parts.append(oh * lax.rsqrt(ms + EPS) * gw_ref[...] * gate[:, sl])
    hg = jnp.concatenate(parts, axis=1).astype(BF16)
    y = _dot(hg, w_ref[0:HG_W, :]) + _dot(ona_ref[...], w_ref[HG_W:, :])
    x = x_ref[...] + y
    xo_ref[...] = x
    if len(rest) > 1:
        nw_ref, whi_ref, wlo_ref, _, ids_ref, wts_ref = rest
        _route(x, nw_ref, whi_ref, wlo_ref, ids_ref, wts_ref)


def _outproj(o_f, o_b, uhg, o_na, x, gw, w, route=None):
    n, d = x.shape
    tok = lambda width, col=0: pl.BlockSpec((TM, width), lambda i: (i, col))
    in_specs = [tok(HG_W), tok(HG_W), tok(HG_W, 4), tok(NA_W), tok(d), _resident(gw.shape), _resident(w.shape)]
    out_specs = [tok(d)]
    out_shape = [jax.ShapeDtypeStruct((n, d), F32)]
    args = [o_f, o_b, uhg, o_na, x, gw, w]
    if route is not None:
        in_specs += [_resident(a.shape) for a in route]
        args += list(route)
        out_specs += [tok(LANES), tok(LANES)]
        out_shape += [jax.ShapeDtypeStruct((n, LANES), jnp.int32), jax.ShapeDtypeStruct((n, LANES), F32)]
    out = pl.pallas_call(
        _outproj_body,
        grid=(n // TM,),
        in_specs=in_specs,
        out_specs=out_specs,
        out_shape=out_shape,
        compiler_params=_params("arbitrary"),
        name="outproj_route" if route is not None else "outproj",
    )(*args)
    return out[0] if route is None else out


def _ffn_body(te_ref, nv_ref, x_ref, nw_ref, wg_ref, wu_ref, wd_ref, o_ref, xn_ref, *, residual):
    del te_ref
    i = pl.program_id(0)
    f = pl.program_id(1)
    valid = i < nv_ref[0]

    @pl.when(jnp.logical_and(f == 0, valid))
    def _():
        xr = x_ref[...]
        ms = jnp.mean(xr * xr, axis=-1, keepdims=True)
        xn_ref[...] = (xr * lax.rsqrt(ms + EPS) * nw_ref[...]).astype(BF16)
        o_ref[...] = xr if residual else jnp.zeros_like(xr)

    @pl.when(jnp.logical_and(f == 0, jnp.logical_not(valid)))
    def _():
        o_ref[...] = jnp.zeros_like(o_ref)

    @pl.when(valid)
    def _():
        x = xn_ref[...]
        tf = wg_ref.shape[2]
        for c0 in range(0, tf, FFN_SUB):
            c1 = min(c0 + FFN_SUB, tf)
            gate = _dot(x, wg_ref[0, :, c0:c1])
            up = _dot(x, wu_ref[0, :, c0:c1])
            hid = (gate / (1.0 + jnp.exp(-gate)) * up).astype(BF16)
            o_ref[...] += _dot(hid, wd_ref[0, c0:c1, :])


def _ffn(x, nw, wg, wu, wd, tile_expert, n_valid, tm, tf, residual):
    p, d = x.shape
    ff = wg.shape[2]
    assert ff % tf == 0 and tf % MXU_N == 0
    once = dict(pipeline_mode=pl.Buffered(1)) if (wg.shape[0] == 1 and tf == ff) else {}
    row = pl.BlockSpec((tm, d), lambda i, f, te, nv: (i, 0))
    grid_spec = pltpu.PrefetchScalarGridSpec(
        num_scalar_prefetch=2,
        grid=(p // tm, ff // tf),
        in_specs=[row,
                  pl.BlockSpec(nw.shape, lambda i, f, te, nv: (0, 0)),
                  pl.BlockSpec((1, d, tf), lambda i, f, te, nv: (te[i], 0, f), **once),
                  pl.BlockSpec((1, d, tf), lambda i, f, te, nv: (te[i], 0, f), **once),
                  pl.BlockSpec((1, tf, d), lambda i, f, te, nv: (te[i], f, 0), **once)],
        out_specs=row,
        scratch_shapes=[pltpu.VMEM((tm, d), BF16)],
    )
    return pl.pallas_call(
        functools.partial(_ffn_body, residual=residual),
        grid_spec=grid_spec,
        out_shape=jax.ShapeDtypeStruct((p, d), F32),
        compiler_params=pltpu.CompilerParams(dimension_semantics=("arbitrary", "arbitrary"),
                                             vmem_limit_bytes=FFN_VMEM_LIMIT),
        name="ffn_res" if residual else "ffn_grouped",
    )(tile_expert, n_valid, x, nw, wg, wu, wd)


def _row_copy(src_ref, dst_ref, sem, src_row, dst_row):
    return pltpu.make_async_copy(src_ref.at[pl.ds(src_row, 1)], dst_ref.at[pl.ds(dst_row, 1)], sem)


def _gather_into(idx_smem, src_hbm, dst_ref, sem):
    def start(b, carry):
        for u in range(DMA_UNROLL):
            r = b * DMA_UNROLL + u
            _row_copy(src_hbm, dst_ref, sem, idx_smem[r], r).start(priority=u % 2)
        return carry

    lax.fori_loop(0, GATHER_ROWS // DMA_UNROLL, start, 0)

    def wait(b, carry):
        for u in range(DMA_UNROLL):
            _row_copy(src_hbm, dst_ref, sem, 0, b * DMA_UNROLL + u).wait()
        return carry

    lax.fori_loop(0, GATHER_ROWS // DMA_UNROLL, wait, 0)


def _gather_body(idx_hbm, src_hbm, o_ref, idx_smem, sem_idx, sem_rows):
    i = pl.program_id(0)
    cp = pltpu.make_async_copy(idx_hbm.at[pl.ds(i * GATHER_ROWS, GATHER_ROWS)], idx_smem, sem_idx)
    cp.start()
    cp.wait()
    _gather_into(idx_smem, src_hbm, o_ref, sem_rows)


def _gather_rows(idx, src):
    p = idx.shape[0]
    w = src.shape[1]
    return pl.pallas_call(
        _gather_body,
        grid=(p // GATHER_ROWS,),
        in_specs=[pl.BlockSpec(memory_space=pl.ANY), pl.BlockSpec(memory_space=pl.ANY)],
        out_specs=pl.BlockSpec((GATHER_ROWS, w), lambda i: (i, 0)),
        out_shape=jax.ShapeDtypeStruct((p, w), src.dtype),
        scratch_shapes=[pltpu.SMEM((GATHER_ROWS,), jnp.int32),
                        pltpu.SemaphoreType.DMA(()), pltpu.SemaphoreType.DMA(())],
        compiler_params=_params("arbitrary"),
        name="gather_rows",
    )(idx, src)


def _combine_body(pos_hbm, y_hbm, x_ref, wts_ref, o_ref, pos_smem, buf_ref, sem_idx, sem_rows):
    i = pl.program_id(0)
    tc = GATHER_ROWS // TOP_K
    cp = pltpu.make_async_copy(pos_hbm.at[pl.ds(i * GATHER_ROWS, GATHER_ROWS)], pos_smem, sem_idx)
    cp.start()
    cp.wait()
    _gather_into(pos_smem, y_hbm, buf_ref, sem_rows)
    w = wts_ref[...]
    o_ref[...] = x_ref[...] + (w[:, 0:1] * buf_ref[0:tc, :] + w[:, 1:2] * buf_ref[tc:2 * tc, :])


def _combine(pos, y, x, wts):
    n, d = x.shape
    tc = GATHER_ROWS // TOP_K
    return pl.pallas_call(
        _combine_body,
        grid=(n // tc,),
        in_specs=[pl.BlockSpec(memory_space=pl.ANY), pl.BlockSpec(memory_space=pl.ANY),
                  pl.BlockSpec((tc, d), lambda i: (i, 0)), pl.BlockSpec((tc, LANES), lambda i: (i, 0))],
        out_specs=pl.BlockSpec((tc, d), lambda i: (i, 0)),
        out_shape=jax.ShapeDtypeStruct((n, d), F32),
        scratch_shapes=[pltpu.SMEM((GATHER_ROWS,), jnp.int32), pltpu.VMEM((GATHER_ROWS, d), F32),
                        pltpu.SemaphoreType.DMA(()), pltpu.SemaphoreType.DMA(())],
        compiler_params=_params("arbitrary"),
        name="combine",
    )(pos, y, x, wts)


def _moe(x, ids, wts, nw, wg, wu, wd):
    n, d = x.shape
    e_flat = ids[:, :TOP_K].reshape(-1)
    na = n * TOP_K
    onehot = (e_flat[:, None] == jnp.arange(N_EXPERTS, dtype=jnp.int32)[None, :]).astype(jnp.int32)
    csum = jnp.cumsum(onehot, axis=0)
    counts = csum[-1]
    rank = jnp.sum(csum * onehot, axis=1) - 1
    padded = (counts + FFN_TM - 1) // FFN_TM * FFN_TM
    pad_end = jnp.cumsum(padded)
    pad_off = pad_end - padded
    raw_off = jnp.cumsum(counts) - counts
    pos = jnp.sum(onehot * pad_off[None, :], axis=1) + rank
    n_rows = na + N_EXPERTS * (FFN_TM - 1)
    n_rows = -(-n_rows // GATHER_ROWS) * GATHER_ROWS
    tile_start = jnp.arange(n_rows // FFN_TM, dtype=jnp.int32) * FFN_TM
    tile_expert = jnp.minimum(
        jnp.sum(tile_start[:, None] >= pad_end[None, :], axis=1), N_EXPERTS - 1).astype(jnp.int32)
    n_valid = (pad_end[-1] // FFN_TM).astype(jnp.int32).reshape(1)
    order = jnp.argsort(e_flat, stable=True).astype(jnp.int32)
    row = jnp.arange(n_rows, dtype=jnp.int32)
    e_row = jnp.repeat(tile_expert, FFN_TM)
    local = row - pad_off[e_row]
    src = jnp.clip(raw_off[e_row] + local, 0, na - 1)
    src_tok = jnp.where(local < counts[e_row], order[src] // TOP_K, 0).astype(jnp.int32)

    xs = _gather_rows(src_tok, x)
    y = _ffn(xs, nw, wg, wu, wd, tile_expert, n_valid, tm=FFN_TM, tf=wg.shape[2] // 2, residual=False)
    tc = GATHER_ROWS // TOP_K
    pos_tiles = pos.astype(jnp.int32).reshape(n // tc, tc, TOP_K).transpose(0, 2, 1).reshape(-1)
    return _combine(pos_tiles, y, x, wts)


def kernel(x_prompt, x_sample, norm_mix, w_in, lb_logits, hg_gnorm, na_qnorm, na_knorm, na_rpb, w_out, norm_ffn,
           ffn_gate, ffn_up, ffn_down, router, exp_gate, exp_up, exp_down):
    bp, tp, d = x_prompt.shape
    bs, ts, _ = x_sample.shape
    depth = w_in.shape[0]
    seq_lens = [tp] * bp + [ts] * bs
    assert tp % HG_TB == 0 and ts % HG_TB == 0 and min(tp, ts) // GRID_W >= WIN_H
    n = bp * tp + bs * ts
    assert n % FFN_TM == 0 and GATHER_ROWS % FFN_TM == 0 and n % (GATHER_ROWS // TOP_K) == 0
    x = jnp.concatenate([x_prompt.reshape(bp * tp, d), x_sample.reshape(bs * ts, d)], axis=0)

    lb = jnp.cumsum(jax.nn.softmax(lb_logits.astype(F32), axis=1), axis=1)
    lb = lb - lb[:, :1]
    lbp = jnp.stack([jnp.log(lb), jnp.log1p(-lb)], axis=2)

    seg = jnp.asarray(np.kron(np.eye(MXU_N // NA_DH), np.ones((NA_DH, NA_DH))) / NA_DH, BF16)
    scale = NA_DH ** -0.5

    for l in range(depth):
        qkw = jnp.stack([jnp.tile(na_qnorm[l].astype(F32), NA_HEADS) * scale,
                         jnp.tile(na_knorm[l].astype(F32), NA_HEADS)])
        uhg, una = _inproj(x, norm_mix[l].reshape(1, d), w_in[l].astype(BF16), qkw, seg)
        o_f, o_b = _hgrn(uhg, lbp[0, l], lbp[1, l], seq_lens)
        o_na = _natten(una, _na_bias_table(na_rpb[l]), seq_lens)
        nw = norm_ffn[l].reshape(1, d)
        j = l // 2
        gw = hg_gnorm[l].reshape(1, HG_D)
        if l % 2 == 0:
            x = _outproj(o_f, o_b, uhg, o_na, x, gw, w_out[l].astype(BF16))
            n_tiles = n // FFN_TM
            x = _ffn(x, nw, ffn_gate[j:j + 1].astype(BF16), ffn_up[j:j + 1].astype(BF16),
                     ffn_down[j:j + 1].astype(BF16), jnp.zeros((n_tiles,), jnp.int32),
                     jnp.full((1,), n_tiles, jnp.int32), tm=FFN_TM, tf=ffn_gate.shape[2], residual=True)
        else:
            wr = jnp.pad(router[j].astype(F32), ((0, 0), (0, LANES - N_EXPERTS)))
            whi = wr.astype(BF16)
            wlo = (wr - whi.astype(F32)).astype(BF16)
            x, ids, wts = _outproj(o_f, o_b, uhg, o_na, x, gw, w_out[l].astype(BF16), route=(nw, whi, wlo))
            x = _moe(x, ids, wts, nw, exp_gate[j].astype(BF16), exp_up[j].astype(BF16), exp_down[j].astype(BF16))

    y_prompt = x[:bp * tp].reshape(bp, tp, d)
    y_sample = x[bp * tp:].reshape(bs, ts, d)
    return (y_prompt, y_sample)
```

```python
import functools

import numpy as np
import jax
import jax.numpy as jnp
from jax import lax
from jax.experimental import pallas as pl
from jax.experimental.pallas import tpu as pltpu

F32 = jnp.float32
BF16 = jnp.bfloat16

GRID_W = 64
HG_HEADS, HG_D = 4, 128
HG_W = HG_HEADS * HG_D
NA_HEADS, NA_DH = 8, 64
NA_W = NA_HEADS * NA_DH
WIN_H, WIN_W = 8, 16
N_EXPERTS, TOP_K = 8, 2
EPS = 1e-6
NEG = -1e30

LANES = 128
MXU_N = 256
DMA_UNROLL = 8
VMEM_LIMIT = 48 * 1024 * 1024
TM = 512
HG_TB = 512
HG_C = 128
HG_MAX_EXPONENT = 80.0
NA_RB_CHOICES = (32, 16, 8)
NA_QUAD = 4
NA_UNION = WIN_H + NA_QUAD
FFN_TM = 1024
FFN_SUB = 2 * MXU_N
FFN_VMEM_LIMIT = 54 * 1024 * 1024
GATHER_ROWS = 2048


def _dot(a, b):
    return jnp.dot(a, b, preferred_element_type=F32)


def _dot_nt(a, b):
    return lax.dot_general(a, b, (((1,), (1,)), ((), ())), preferred_element_type=F32)


def _dot_tn(a, b):
    return lax.dot_general(a, b, (((0,), (0,)), ((), ())), preferred_element_type=F32)


def _params(*sem):
    return pltpu.CompilerParams(dimension_semantics=sem, vmem_limit_bytes=VMEM_LIMIT)


def _resident(shape):
    nd = len(shape)
    return pl.BlockSpec(shape, lambda *_: (0,) * nd, pipeline_mode=pl.Buffered(1))


def _inproj_body(x_ref, nw_ref, w_ref, qkw_ref, seg_ref, uhg_ref, una_ref):
    x = x_ref[...]
    ms = jnp.mean(x * x, axis=-1, keepdims=True)
    h = (x * lax.rsqrt(ms + EPS) * nw_ref[...]).astype(BF16)
    n_hg = uhg_ref.shape[1]
    uhg_ref[...] = _dot(h, w_ref[:, 0:n_hg])
    una_ref[:, 2 * NA_W:] = _dot(h, w_ref[:, n_hg + 2 * NA_W:]).astype(BF16)
    seg = seg_ref[...]
    cw = seg.shape[0]
    cols = [(part, c) for part in range(2) for c in range(0, NA_W, cw)]
    ys = [_dot(h, w_ref[:, n_hg + part * NA_W + c:n_hg + part * NA_W + c + cw]) for part, c in cols]
    for (part, c), y in zip(cols, ys):
        msq = _dot((y * y).astype(BF16), seg)
        y = y * lax.rsqrt(msq + EPS) * qkw_ref[part:part + 1, c:c + cw]
        una_ref[:, part * NA_W + c:part * NA_W + c + cw] = y.astype(BF16)


def _inproj(x, nw, w, qkw, seg):
    n, d = x.shape
    n_hg = w.shape[1] - 3 * NA_W
    return pl.pallas_call(
        _inproj_body,
        grid=(n // TM,),
        in_specs=[
            pl.BlockSpec((TM, d), lambda i: (i, 0)),
            _resident(nw.shape), _resident(w.shape), _resident(qkw.shape), _resident(seg.shape),
        ],
        out_specs=[
            pl.BlockSpec((TM, n_hg), lambda i: (i, 0)),
            pl.BlockSpec((TM, 3 * NA_W), lambda i: (i, 0)),
        ],
        out_shape=[
            jax.ShapeDtypeStruct((n, n_hg), F32),
            jax.ShapeDtypeStruct((n, 3 * NA_W), BF16),
        ],
        compiler_params=_params("arbitrary"),
        name="inproj",
    )(x, nw, w, qkw, seg)


def _split2(x):
    hi = x.astype(BF16)
    lo = (x - hi.astype(F32)).astype(BF16)
    return hi, lo


def _hgrn_prep(q_ref, f_ref, lb_ref, tri_ref, qs_ref, ks_ref, bs_ref, r0, reverse):
    c = HG_C
    rows = pl.ds(r0, c)
    q = q_ref[rows, :]
    x = f_ref[rows, :]
    la = lb_ref[0:1, :]
    l1 = lb_ref[1:2, :]
    ls = jnp.minimum(x, 0.0) - jnp.log(1.0 + jnp.exp(-jnp.abs(x)))
    cc = l1 + ls
    g = jnp.maximum(la, cc) + jnp.log(1.0 + jnp.exp(-jnp.abs(la - cc)))
    bc = _dot(tri_ref[...], jnp.concatenate(_split2(g), axis=1))
    b = bc[:, :HG_W] + bc[:, HG_W:]
    qs = q / (1.0 + jnp.exp(-q))
    qs_ref[rows, :] = qs
    ks_ref[rows, :] = jnp.exp(cc - x)
    bs_ref[rows, :] = b
    m = c // 2
    if reverse:
        near, total = b[m:m + 1, :], b[0:1, :]
    else:
        near, total = b[m - 1:m, :], b[c - 1:c, :]
    qmax = jnp.max(jnp.abs(qs), axis=0, keepdims=True)
    return jnp.maximum(-near, near - total) + jnp.log(jnp.maximum(qmax, 1.0))


def _hgrn_chunk(q_ref, k_ref, b_ref, v_ref, o_ref, st_ref, r0, reverse, fast):
    c = HG_C
    rows = pl.ds(r0, c)
    q = q_ref[rows, :]
    k = k_ref[rows, :]
    b = b_ref[rows, :]
    v = v_ref[rows, :]

    row = lax.broadcasted_iota(jnp.int32, (c, HG_D), 0)
    ri = lax.broadcasted_iota(jnp.int32, (c, c), 0)
    ci = lax.broadcasted_iota(jnp.int32, (c, c), 1)
    nb = c // 8
    i8 = lax.broadcasted_iota(jnp.int32, (nb, 8, HG_D), 1)

    for h in range(HG_HEADS):
        sl = slice(h * HG_D, (h + 1) * HG_D)
        bh, qh, kh, vh = b[:, sl], q[:, sl], k[:, sl], v[:, sl]
        vb = vh.astype(BF16)
        st = st_ref[h]
        btot = bh[0:1, :] if reverse else bh[c - 1:c, :]
        qe = (qh * jnp.exp(bh)).astype(BF16)
        ke = (kh * jnp.exp(btot - bh)).astype(BF16)
        o = _dot_nt(qe, st.astype(BF16))
        st_ref[h] = st * jnp.exp(btot) + _dot_tn(vb, ke)

        if fast:
            m = c // 2
            r = bh[m:m + 1, :] if reverse else bh[m - 1:m, :]
            qr = (qh * jnp.exp(bh - r)).astype(BF16)
            kr = (kh * jnp.exp(r - bh)).astype(BF16)
            causal = (ri <= ci) if reverse else (ri >= ci)
            a = jnp.where(causal, _dot_nt(qr, kr), 0.0)
            o_ref[rows, sl] = o + _dot(a.astype(BF16), vb)
            continue

        a = jnp.zeros((c, c), F32)
        w = c // 2
        while w >= 8:
            grp = c // (2 * w)
            b4 = bh.reshape(grp, 2 * w, HG_D)
            edge = w if reverse else w - 1
            r = jnp.broadcast_to(b4[:, edge:edge + 1, :], (grp, 2 * w, HG_D)).reshape(c, HG_D)
            e = jnp.exp(-jnp.abs(bh - r))
            late = (row & (2 * w - 1)) >= w
            is_q = jnp.logical_not(late) if reverse else late
            qs = jnp.where(is_q, qh * e, 0.0).astype(BF16)
            ks = jnp.where(is_q, 0.0, kh * e).astype(BF16)
            aw = _dot_nt(qs, ks)
            if grp > 1:
                sh = int(np.log2(2 * w))
                aw = jnp.where((ri >> sh) == (ci >> sh), aw, 0.0)
            a = a + aw
            w //= 2
        o = o + _dot(a.astype(BF16), vb)

        b3, q3, k3, v3 = (t.reshape(nb, 8, HG_D) for t in (bh, qh, kh, vh))
        od = jnp.zeros((nb, 8, HG_D), F32)
        for j in range(8):
            valid = (i8 <= j) if reverse else (i8 >= j)
            e = jnp.exp(jnp.where(valid, b3 - b3[:, j:j + 1, :], NEG))
            aj = jnp.sum(q3 * e * k3[:, j:j + 1, :], axis=-1, keepdims=True)
            od = od + aj * v3[:, j:j + 1, :]
        o_ref[rows, sl] = o + od.reshape(c, HG_D)


def _hgrn_body(fwd_ref, bwd_ref, first_ref, qf_ref, ff_ref, vf_ref, qb_ref, fb_ref, vb_ref,
               lbf_ref, lbb_ref, tril_ref, triu_ref, of_ref, ob_ref,
               stf_ref, stb_ref, qsf_ref, ksf_ref, bsf_ref, qsb_ref, ksb_ref, bsb_ref):
    del fwd_ref, bwd_ref
    g = pl.program_id(0)

    @pl.when(first_ref[g] == 1)
    def _():
        stf_ref[...] = jnp.zeros_like(stf_ref)
        stb_ref[...] = jnp.zeros_like(stb_ref)

    n_chunks = HG_TB // HG_C

    def prep(ci, worst):
        r0 = pl.multiple_of(ci * HG_C, HG_C)
        df = _hgrn_prep(qf_ref, ff_ref, lbf_ref, tril_ref, qsf_ref, ksf_ref, bsf_ref, r0, False)
        db = _hgrn_prep(qb_ref, fb_ref, lbb_ref, triu_ref, qsb_ref, ksb_ref, bsb_ref, r0, True)
        return jnp.maximum(worst, jnp.maximum(df, db))

    worst = jnp.max(lax.fori_loop(0, n_chunks, prep, jnp.zeros((1, HG_W), F32)))
    fits = worst <= HG_MAX_EXPONENT

    def run(fast):
        def body(ci, carry):
            rf = pl.multiple_of(ci * HG_C, HG_C)
            _hgrn_chunk(qsf_ref, ksf_ref, bsf_ref, vf_ref, of_ref, stf_ref, rf, False, fast)
            rb = pl.multiple_of((n_chunks - 1 - ci) * HG_C, HG_C)
            _hgrn_chunk(qsb_ref, ksb_ref, bsb_ref, vb_ref, ob_ref, stb_ref, rb, True, fast)
            return carry

        lax.fori_loop(0, n_chunks, body, 0, unroll=2 if fast else 1)

    @pl.when(fits)
    def _():
        run(True)

    @pl.when(jnp.logical_not(fits))
    def _():
        run(False)


def _hgrn(uhg, lbf, lbb, seq_lens):
    n = uhg.shape[0]
    fwd, bwd, first = [], [], []
    base = 0
    for t in seq_lens:
        nt = t // HG_TB
        for j in range(nt):
            fwd.append(base + j)
            bwd.append(base + nt - 1 - j)
            first.append(1 if j == 0 else 0)
        base += nt
    fwd, bwd, first = (jnp.asarray(np.array(a, np.int32)) for a in (fwd, bwd, first))
    tril = jnp.asarray(np.tril(np.ones((HG_C, HG_C), np.float32)), BF16)
    triu = jnp.asarray(np.triu(np.ones((HG_C, HG_C), np.float32)), BF16)

    def spec(which, col):
        if which == 0:
            return pl.BlockSpec((HG_TB, HG_W), lambda g, f, b, s: (f[g], col))
        return pl.BlockSpec((HG_TB, HG_W), lambda g, f, b, s: (b[g], col))

    const = lambda shape: pl.BlockSpec(shape, lambda g, f, b, s: (0, 0))
    grid_spec = pltpu.PrefetchScalarGridSpec(
        num_scalar_prefetch=3,
        grid=(n // HG_TB,),
        in_specs=[spec(0, 0), spec(0, 1), spec(0, 3), spec(1, 0), spec(1, 2), spec(1, 3),
                  const(lbf.shape), const(lbb.shape), const(tril.shape), const(triu.shape)],
        out_specs=[spec(0, 0), spec(1, 0)],
        scratch_shapes=[pltpu.VMEM((HG_HEADS, HG_D, HG_D), F32)] * 2 + [pltpu.VMEM((HG_TB, HG_W), F32)] * 6,
    )
    return pl.pallas_call(
        _hgrn_body,
        grid_spec=grid_spec,
        out_shape=[jax.ShapeDtypeStruct((n, HG_W), F32), jax.ShapeDtypeStruct((n, HG_W), F32)],
        compiler_params=_params("arbitrary"),
        name="hgrn",
    )(fwd, bwd, first, uhg, uhg, uhg, uhg, uhg, uhg, lbf, lbb, tril, triu)


def _na_body(prev_ref, next_ref, rloc_ref, rtot_ref, q_ref, kp_ref, kc_ref, kn_ref,
             vp_ref, vc_ref, vn_ref, bias_ref, o_ref, ks_ref, vs_ref):
    del prev_ref, next_ref
    g = pl.program_id(1)
    blk = q_ref.shape[0]
    rb = blk // GRID_W
    ks_ref[0:blk, :] = kp_ref[...]
    ks_ref[blk:2 * blk, :] = kc_ref[...]
    ks_ref[2 * blk:3 * blk, :] = kn_ref[...]
    vs_ref[0:blk, :] = vp_ref[...]
    vs_ref[blk:2 * blk, :] = vc_ref[...]
    vs_ref[2 * blk:3 * blk, :] = vn_ref[...]
    r0 = rloc_ref[g]
    n_rows = rtot_ref[g]
    first_head = lax.broadcasted_iota(jnp.int32, (GRID_W, LANES), 1) < NA_DH
    nkeys = NA_UNION * GRID_W
    qrows = 2 * GRID_W

    for i0 in range(0, rb, NA_QUAD):
        rs0 = jnp.clip(r0 + i0 - WIN_H // 2, 0, n_rows - WIN_H)
        off = pl.multiple_of((rs0 - r0 + rb) * GRID_W, GRID_W)
        kw = ks_ref[pl.ds(off, nkeys), :]
        vw = vs_ref[pl.ds(off, nkeys), :]
        pieces = []
        for a in range(NA_QUAD):
            q2 = q_ref[(i0 + a) * GRID_W:(i0 + a + 1) * GRID_W, :]
            zero = jnp.zeros_like(q2)
            pieces += [jnp.where(first_head, q2, zero), jnp.where(first_head, zero, q2)]
        s = _dot_nt(jnp.concatenate(pieces, axis=0), kw)
        probs, sums = [], []
        for a in range(NA_QUAD):
            r = r0 + i0 + a
            rs = jnp.clip(r - WIN_H // 2, 0, n_rows - WIN_H)
            rel = rs - rs0
            entry = jnp.where(rel > 0, WIN_H - 1 + rel, rs - r + WIN_H - 1)
            sa = s[a * qrows:(a + 1) * qrows] + bias_ref[0, entry]
            p = jnp.exp(sa - jnp.max(sa, axis=-1, keepdims=True))
            sums.append(jnp.sum(p, axis=-1, keepdims=True))
            probs.append(p.astype(BF16))
        ov = _dot(jnp.concatenate(probs, axis=0), vw)
        for a in range(NA_QUAD):
            oa = ov[a * qrows:(a + 1) * qrows] / sums[a]
            out = jnp.where(first_head, oa[:GRID_W], oa[GRID_W:])
            o_ref[(i0 + a) * GRID_W:(i0 + a + 1) * GRID_W, :] = out.astype(o_ref.dtype)


def _natten(una, bias, seq_lens):
    n = una.shape[0]
    rb = max(c for c in NA_RB_CHOICES if all((t // GRID_W) % c == 0 for t in seq_lens))
    blk = rb * GRID_W
    prev, nxt, rloc, rtot = [], [], [], []
    base = 0
    for t in seq_lens:
        rows = t // GRID_W
        nb = rows // rb
        for j in range(nb):
            prev.append(base + max(j - 1, 0))
            nxt.append(base + min(j + 1, nb - 1))
            rloc.append(j * rb)
            rtot.append(rows)
        base += nb
    prev, nxt, rloc, rtot = (jnp.asarray(np.array(a, np.int32)) for a in (prev, nxt, rloc, rtot))
    npair = NA_W // LANES

    def cur(part):
        return pl.BlockSpec((blk, LANES), lambda p, g, pv, nx, rl, rt: (g, part * npair + p))

    def nbr(part, which):
        if which == 0:
            return pl.BlockSpec((blk, LANES), lambda p, g, pv, nx, rl, rt: (pv[g], part * npair + p))
        return pl.BlockSpec((blk, LANES), lambda p, g, pv, nx, rl, rt: (nx[g], part * npair + p))

    grid_spec = pltpu.PrefetchScalarGridSpec(
        num_scalar_prefetch=4,
        grid=(npair, n // blk),
        in_specs=[cur(0), nbr(1, 0), cur(1), nbr(1, 1), nbr(2, 0), cur(2), nbr(2, 1),
                  pl.BlockSpec((1,) + bias.shape[1:], lambda p, g, pv, nx, rl, rt: (p, 0, 0, 0))],
        out_specs=pl.BlockSpec((blk, LANES), lambda p, g, pv, nx, rl, rt: (g, p)),
        scratch_shapes=[pltpu.VMEM((3 * blk, LANES), BF16), pltpu.VMEM((3 * blk, LANES), BF16)],
    )
    return pl.pallas_call(
        _na_body,
        grid_spec=grid_spec,
        out_shape=jax.ShapeDtypeStruct((n, NA_W), BF16),
        compiler_params=_params("arbitrary", "arbitrary"),
        name="natten",
    )(prev, nxt, rloc, rtot, una, una, una, una, una, una, una, bias)


def _na_bias_table(rpb):
    c = np.arange(GRID_W)
    cstart = np.clip(c - WIN_W // 2, 0, GRID_W - WIN_W)
    kc = np.arange(GRID_W)
    mask = (kc[None, :] >= cstart[:, None]) & (kc[None, :] < cstart[:, None] + WIN_W)
    dx = np.clip(kc[None, :] - c[:, None] + WIN_W - 1, 0, 2 * WIN_W - 2)
    onehot = np.zeros((2 * WIN_W - 1, GRID_W * GRID_W), np.float32)
    onehot[dx.reshape(-1), np.arange(GRID_W * GRID_W)] = mask.reshape(-1)
    t = jnp.einsum('hyx,xq->hyq', rpb.astype(F32), jnp.asarray(onehot), precision=lax.Precision.HIGHEST)
    t = t + jnp.asarray(np.where(mask.reshape(-1), 0.0, NEG).astype(np.float32))
    def entry(o, rel):
        neg = lambda k: jnp.full((NA_HEADS, k, GRID_W * GRID_W), NEG, F32)
        return jnp.concatenate([neg(rel), t[:, o:o + WIN_H], neg(NA_UNION - WIN_H - rel)], axis=1)

    t = jnp.stack([entry(o, 0) for o in range(WIN_H)]
                  + [entry(WIN_H // 2 - 1, rel) for rel in range(1, NA_QUAD)], axis=1)
    ne = t.shape[1]
    t = t.reshape(NA_HEADS, ne, NA_UNION, GRID_W, GRID_W)
    t = t.transpose(0, 1, 3, 2, 4).reshape(NA_HEADS, ne, GRID_W, NA_UNION * GRID_W)
    t = t.reshape(NA_HEADS // 2, 2, ne, GRID_W, NA_UNION * GRID_W).transpose(0, 2, 1, 3, 4)
    return t.reshape(NA_HEADS // 2, ne, 2 * GRID_W, NA_UNION * GRID_W)


def _route(x, nw_ref, whi_ref, wlo_ref, ids_ref, wts_ref):
    ms = jnp.mean(x * x, axis=-1, keepdims=True)
    h = x * lax.rsqrt(ms + EPS) * nw_ref[...]
    hh = h.astype(BF16)
    hl = (h - hh.astype(F32)).astype(BF16)
    logits = _dot(hh, whi_ref[...]) + (_dot(hl, whi_ref[...]) + _dot(hh, wlo_ref[...]))
    col = lax.broadcasted_iota(jnp.int32, logits.shape, 1).astype(F32)
    logits = jnp.where(col < N_EXPERTS, logits, -jnp.inf)
    m1 = jnp.max(logits, axis=-1, keepdims=True)
    i1 = jnp.min(jnp.where(logits == m1, col, float(LANES)), axis=-1, keepdims=True)
    rest = jnp.where(col == i1, -jnp.inf, logits)
    m2 = jnp.max(rest, axis=-1, keepdims=True)
    i2 = jnp.min(jnp.where(rest == m2, col, float(LANES)), axis=-1, keepdims=True)
    e = jnp.exp(m2 - m1)
    w1 = 1.0 / (1.0 + e)
    w2 = e / (1.0 + e)
    ids_ref[...] = jnp.where(col == 0.0, i1, jnp.where(col == 1.0, i2, 0.0)).astype(jnp.int32)
    wts_ref[...] = jnp.where(col == 0.0, w1, jnp.where(col == 1.0, w2, 0.0))


def _outproj_body(of_ref, ob_ref, g_ref, ona_ref, x_ref, gw_ref, w_ref, *rest):
    xo_ref = rest[-1] if len(rest) == 1 else rest[3]
    o = of_ref[...] + ob_ref[...]
    g = g_ref[...]
    gate = g / (1.0 + jnp.exp(-g))
    parts = []
    for h in range(HG_HEADS):
        sl = slice(h * HG_D, (h + 1) * HG_D)
        oh = o[:, sl]
        ms = jnp.mean(oh * oh, axis=-1, keepdims=True)
        parts.append(oh * lax.rsqrt(ms + EPS) * gw_ref[...] * gate[:, sl])
    hg = jnp.concatenate(parts, axis=1).astype(BF16)
    y = _dot(hg, w_ref[0:HG_W, :]) + _dot(ona_ref[...], w_ref[HG_W:, :])
    x = x_ref[...] + y
    xo_ref[...] = x
    if len(rest) > 1:
        nw_ref, whi_ref, wlo_ref, _, ids_ref, wts_ref = rest
        _route(x, nw_ref, whi_ref, wlo_ref, ids_ref, wts_ref)


def _outproj(o_f, o_b, uhg, o_na, x, gw, w, route=None):
    n, d = x.shape
    tok = lambda width, col=0: pl.BlockSpec((TM, width), lambda i: (i, col))
    in_specs = [tok(HG_W), tok(HG_W), tok(HG_W, 4), tok(NA_W), tok(d), _resident(gw.shape), _resident(w.shape)]
    out_specs = [tok(d)]
    out_shape = [jax.ShapeDtypeStruct((n, d), F32)]
    args = [o_f, o_b, uhg, o_na, x, gw, w]
    if route is not None:
        in_specs += [_resident(a.shape) for a in route]
        args += list(route)
        out_specs += [tok(LANES), tok(LANES)]
        out_shape += [jax.ShapeDtypeStruct((n, LANES), jnp.int32), jax.ShapeDtypeStruct((n, LANES), F32)]
    out = pl.pallas_call(
        _outproj_body,
        grid=(n // TM,),
        in_specs=in_specs,
        out_specs=out_specs,
        out_shape=out_shape,
        compiler_params=_params("arbitrary"),
        name="outproj_route" if route is not None else "outproj",
    )(*args)
    return out[0] if route is None else out


def _ffn_body(te_ref, nv_ref, x_ref, nw_ref, wg_ref, wu_ref, wd_ref, o_ref, xn_ref, *, residual):
    del te_ref
    i = pl.program_id(0)
    f = pl.program_id(1)
    valid = i < nv_ref[0]

    @pl.when(jnp.logical_and(f == 0, valid))
    def _():
        xr = x_ref[...]
        ms = jnp.mean(xr * xr, axis=-1, keepdims=True)
        xn_ref[...] = (xr * lax.rsqrt(ms + EPS) * nw_ref[...]).astype(BF16)
        o_ref[...] = xr if residual else jnp.zeros_like(xr)

    @pl.when(jnp.logical_and(f == 0, jnp.logical_not(valid)))
    def _():
        o_ref[...] = jnp.zeros_like(o_ref)

    @pl.when(valid)
    def _():
        x = xn_ref[...]
        tf = wg_ref.shape[2]
        for c0 in range(0, tf, FFN_SUB):
            c1 = min(c0 + FFN_SUB, tf)
            gate = _dot(x, wg_ref[0, :, c0:c1])
            up = _dot(x, wu_ref[0, :, c0:c1])
            hid = (gate / (1.0 + jnp.exp(-gate)) * up).astype(BF16)
            o_ref[...] += _dot(hid, wd_ref[0, c0:c1, :])


def _ffn(x, nw, wg, wu, wd, tile_expert, n_valid, tm, tf, residual):
    p, d = x.shape
    ff = wg.shape[2]
    assert ff % tf == 0 and tf % MXU_N == 0
    once = dict(pipeline_mode=pl.Buffered(1)) if (wg.shape[0] == 1 and tf == ff) else {}
    row = pl.BlockSpec((tm, d), lambda i, f, te, nv: (i, 0))
    grid_spec = pltpu.PrefetchScalarGridSpec(
        num_scalar_prefetch=2,
        grid=(p // tm, ff // tf),
        in_specs=[row,
                  pl.BlockSpec(nw.shape, lambda i, f, te, nv: (0, 0)),
                  pl.BlockSpec((1, d, tf), lambda i, f, te, nv: (te[i], 0, f), **once),
                  pl.BlockSpec((1, d, tf), lambda i, f, te, nv: (te[i], 0, f), **once),
                  pl.BlockSpec((1, tf, d), lambda i, f, te, nv: (te[i], f, 0), **once)],
        out_specs=row,
        scratch_shapes=[pltpu.VMEM((tm, d), BF16)],
    )
    return pl.pallas_call(
        functools.partial(_ffn_body, residual=residual),
        grid_spec=grid_spec,
        out_shape=jax.ShapeDtypeStruct((p, d), F32),
        compiler_params=pltpu.CompilerParams(dimension_semantics=("arbitrary", "arbitrary"),
                                             vmem_limit_bytes=FFN_VMEM_LIMIT),
        name="ffn_res" if residual else "ffn_grouped",
    )(tile_expert, n_valid, x, nw, wg, wu, wd)


def _row_copy(src_ref, dst_ref, sem, src_row, dst_row):
    return pltpu.make_async_copy(src_ref.at[pl.ds(src_row, 1)], dst_ref.at[pl.ds(dst_row, 1)], sem)


def _gather_into(idx_smem, src_hbm, dst_ref, sem):
    def start(b, carry):
        for u in range(DMA_UNROLL):
            r = b * DMA_UNROLL + u
            _row_copy(src_hbm, dst_ref, sem, idx_smem[r], r).start(priority=u % 2)
        return carry

    lax.fori_loop(0, GATHER_ROWS // DMA_UNROLL, start, 0)

    def wait(b, carry):
        for u in range(DMA_UNROLL):
            _row_copy(src_hbm, dst_ref, sem, 0, b * DMA_UNROLL + u).wait()
        return carry

    lax.fori_loop(0, GATHER_ROWS // DMA_UNROLL, wait, 0)


def _gather_body(idx_hbm, src_hbm, o_ref, idx_smem, sem_idx, sem_rows):
    i = pl.program_id(0)
    cp = pltpu.make_async_copy(idx_hbm.at[pl.ds(i * GATHER_ROWS, GATHER_ROWS)], idx_smem, sem_idx)
    cp.start()
    cp.wait()
    _gather_into(idx_smem, src_hbm, o_ref, sem_rows)


def _gather_rows(idx, src):
    p = idx.shape[0]
    w = src.shape[1]
    return pl.pallas_call(
        _gather_body,
        grid=(p // GATHER_ROWS,),
        in_specs=[pl.BlockSpec(memory_space=pl.ANY), pl.BlockSpec(memory_space=pl.ANY)],
        out_specs=pl.BlockSpec((GATHER_ROWS, w), lambda i: (i, 0)),
        out_shape=jax.ShapeDtypeStruct((p, w), src.dtype),
        scratch_shapes=[pltpu.SMEM((GATHER_ROWS,), jnp.int32),
                        pltpu.SemaphoreType.DMA(()), pltpu.SemaphoreType.DMA(())],
        compiler_params=_params("arbitrary"),
        name="gather_rows",
    )(idx, src)


def _combine_body(pos_hbm, y_hbm, x_ref, wts_ref, o_ref, pos_smem, buf_ref, sem_idx, sem_rows):
    i = pl.program_id(0)
    tc = GATHER_ROWS // TOP_K
    cp = pltpu.make_async_copy(pos_hbm.at[pl.ds(i * GATHER_ROWS, GATHER_ROWS)], pos_smem, sem_idx)
    cp.start()
    cp.wait()
    _gather_into(pos_smem, y_hbm, buf_ref, sem_rows)
    w = wts_ref[...]
    o_ref[...] = x_ref[...] + (w[:, 0:1] * buf_ref[0:tc, :] + w[:, 1:2] * buf_ref[tc:2 * tc, :])


def _combine(pos, y, x, wts):
    n, d = x.shape
    tc = GATHER_ROWS // TOP_K
    return pl.pallas_call(
        _combine_body,
        grid=(n // tc,),
        in_specs=[pl.BlockSpec(memory_space=pl.ANY), pl.BlockSpec(memory_space=pl.ANY),
                  pl.BlockSpec((tc, d), lambda i: (i, 0)), pl.BlockSpec((tc, LANES), lambda i: (i, 0))],
        out_specs=pl.BlockSpec((tc, d), lambda i: (i, 0)),
        out_shape=jax.ShapeDtypeStruct((n, d), F32),
        scratch_shapes=[pltpu.SMEM((GATHER_ROWS,), jnp.int32), pltpu.VMEM((GATHER_ROWS, d), F32),
                        pltpu.SemaphoreType.DMA(()), pltpu.SemaphoreType.DMA(())],
        compiler_params=_params("arbitrary"),
        name="combine",
    )(pos, y, x, wts)


def _moe(x, ids, wts, nw, wg, wu, wd):
    n, d = x.shape
    e_flat = ids[:, :TOP_K].reshape(-1)
    na = n * TOP_K
    onehot = (e_flat[:, None] == jnp.arange(N_EXPERTS, dtype=jnp.int32)[None, :]).astype(jnp.int32)
    csum = jnp.cumsum(onehot, axis=0)
    counts = csum[-1]
    rank = jnp.sum(csum * onehot, axis=1) - 1
    padded = (counts + FFN_TM - 1) // FFN_TM * FFN_TM
    pad_end = jnp.cumsum(padded)
    pad_off = pad_end - padded
    raw_off = jnp.cumsum(counts) - counts
    pos = jnp.sum(onehot * pad_off[None, :], axis=1) + rank
    n_rows = na + N_EXPERTS * (FFN_TM - 1)
    n_rows = -(-n_rows // GATHER_ROWS) * GATHER_ROWS
    tile_start = jnp.arange(n_rows // FFN_TM, dtype=jnp.int32) * FFN_TM
    tile_expert = jnp.minimum(
        jnp.sum(tile_start[:, None] >= pad_end[None, :], axis=1), N_EXPERTS - 1).astype(jnp.int32)
    n_valid = (pad_end[-1] // FFN_TM).astype(jnp.int32).reshape(1)
    order = jnp.argsort(e_flat, stable=True).astype(jnp.int32)
    row = jnp.arange(n_rows, dtype=jnp.int32)
    e_row = jnp.repeat(tile_expert, FFN_TM)
    local = row - pad_off[e_row]
    src = jnp.clip(raw_off[e_row] + local, 0, na - 1)
    src_tok = jnp.where(local < counts[e_row], order[src] // TOP_K, 0).astype(jnp.int32)

    xs = _gather_rows(src_tok, x)
    y = _ffn(xs, nw, wg, wu, wd, tile_expert, n_valid, tm=FFN_TM, tf=wg.shape[2] // 2, residual=False)
    tc = GATHER_ROWS // TOP_K
    pos_tiles = pos.astype(jnp.int32).reshape(n // tc, tc, TOP_K).transpose(0, 2, 1).reshape(-1)
    return _combine(pos_tiles, y, x, wts)


def kernel(x_prompt, x_sample, norm_mix, w_in, lb_logits, hg_gnorm, na_qnorm, na_knorm, na_rpb, w_out, norm_ffn,
           ffn_gate, ffn_up, ffn_down, router, exp_gate, exp_up, exp_down):
    bp, tp, d = x_prompt.shape
    bs, ts, _ = x_sample.shape
    depth = w_in.shape[0]
    seq_lens = [tp] * bp + [ts] * bs
    assert tp % HG_TB == 0 and ts % HG_TB == 0 and min(tp, ts) // GRID_W >= WIN_H
    n = bp * tp + bs * ts
    assert n % FFN_TM == 0 and GATHER_ROWS % FFN_TM == 0 and n % (GATHER_ROWS // TOP_K) == 0
    x = jnp.concatenate([x_prompt.reshape(bp * tp, d), x_sample.reshape(bs * ts, d)], axis=0)

    lb = jnp.cumsum(jax.nn.softmax(lb_logits.astype(F32), axis=1), axis=1)
    lb = lb - lb[:, :1]
    lbp = jnp.stack([jnp.log(lb), jnp.log1p(-lb)], axis=2)

    seg = jnp.asarray(np.kron(np.eye(MXU_N // NA_DH), np.ones((NA_DH, NA_DH))) / NA_DH, BF16)
    scale = NA_DH ** -0.5

    for l in range(depth):
        qkw = jnp.stack([jnp.tile(na_qnorm[l].astype(F32), NA_HEADS) * scale,
                         jnp.tile(na_knorm[l].astype(F32), NA_HEADS)])
        uhg, una = _inproj(x, norm_mix[l].reshape(1, d), w_in[l].astype(BF16), qkw, seg)
        o_f, o_b = _hgrn(uhg, lbp[0, l], lbp[1, l], seq_lens)
        o_na = _natten(una, _na_bias_table(na_rpb[l]), seq_lens)
        nw = norm_ffn[l].reshape(1, d)
        j = l // 2
        gw = hg_gnorm[l].reshape(1, HG_D)
        if l % 2 == 0:
            x = _outproj(o_f, o_b, uhg, o_na, x, gw, w_out[l].astype(BF16))
            n_tiles = n // FFN_TM
            x = _ffn(x, nw, ffn_gate[j:j + 1].astype(BF16), ffn_up[j:j + 1].astype(BF16),
                     ffn_down[j:j + 1].astype(BF16), jnp.zeros((n_tiles,), jnp.int32),
                     jnp.full((1,), n_tiles, jnp.int32), tm=FFN_TM, tf=ffn_gate.shape[2], residual=True)
        else:
            wr = jnp.pad(router[j].astype(F32), ((0, 0), (0, LANES - N_EXPERTS)))
            whi = wr.astype(BF16)
            wlo = (wr - whi.astype(F32)).astype(BF16)
            x, ids, wts = _outproj(o_f, o_b, uhg, o_na, x, gw, w_out[l].astype(BF16), route=(nw, whi, wlo))
            x = _moe(x, ids, wts, nw, exp_gate[j].astype(BF16), exp_up[j].astype(BF16), exp_down[j].astype(BF16))

    y_prompt = x[:bp * tp].reshape(bp, tp, d)
    y_sample = x[bp * tp:].reshape(bs, ts, d)
    return (y_prompt, y_sample)
```

```python
import functools

import numpy as np
import jax
import jax.numpy as jnp
from jax import lax
from jax.experimental import pallas as pl
from jax.experimental.pallas import tpu as pltpu

F32 = jnp.float32
BF16 = jnp.bfloat16

GRID_W = 64
HG_HEADS, HG_D = 4, 128
HG_W = HG_HEADS * HG_D
NA_HEADS, NA_DH = 8, 64
NA_W = NA_HEADS * NA_DH
WIN_H, WIN_W = 8, 16
N_EXPERTS, TOP_K = 8, 2
EPS = 1e-6
NEG = -1e30

LANES = 128
MXU_N = 256
DMA_UNROLL = 8
VMEM_LIMIT = 48 * 1024 * 1024
TM = 512
HG_TB = 512
HG_C = 128
HG_MAX_EXPONENT = 80.0
NA_RB_CHOICES = (32, 16, 8)
NA_QUAD = 4
NA_UNION = WIN_H + NA_QUAD
FFN_TM = 1024
FFN_SUB = 2 * MXU_N
FFN_VMEM_LIMIT = 54 * 1024 * 1024
GATHER_ROWS = 2048


def _dot(a, b):
    return jnp.dot(a, b, preferred_element_type=F32)


def _dot_nt(a, b):
    return lax.dot_general(a, b, (((1,), (1,)), ((), ())), preferred_element_type=F32)


def _dot_tn(a, b):
    return lax.dot_general(a, b, (((0,), (0,)), ((), ())), preferred_element_type=F32)


def _params(*sem):
    return pltpu.CompilerParams(dimension_semantics=sem, vmem_limit_bytes=VMEM_LIMIT)


def _resident(shape):
    nd = len(shape)
    return pl.BlockSpec(shape, lambda *_: (0,) * nd, pipeline_mode=pl.Buffered(1))


def _inproj_body(*refs, first_blocks):
    if first_blocks is None:
        x_ref, nw_ref, w_ref, qkw_ref, seg_ref, uhg_ref, una_ref = refs
        x = x_ref[...]
    else:
        xa_ref, xb_ref, nw_ref, w_ref, qkw_ref, seg_ref, uhg_ref, una_ref, xcat_ref = refs
        x = jnp.where(pl.program_id(0) < first_blocks, xa_ref[...], xb_ref[...])
        xcat_ref[...] = x
    ms = jnp.mean(x * x, axis=-1, keepdims=True)
    h = (x * lax.rsqrt(ms + EPS) * nw_ref[...]).astype(BF16)
    n_hg = uhg_ref.shape[1]
    uhg_ref[...] = _dot(h, w_ref[:, 0:n_hg])
    una_ref[:, 2 * NA_W:] = _dot(h, w_ref[:, n_hg + 2 * NA_W:]).astype(BF16)
    seg = seg_ref[...]
    cw = seg.shape[0]
    cols = [(part, c) for part in range(2) for c in range(0, NA_W, cw)]
    ys = [_dot(h, w_ref[:, n_hg + part * NA_W + c:n_hg + part * NA_W + c + cw]) for part, c in cols]
    for (part, c), y in zip(cols, ys):
        msq = _dot((y * y).astype(BF16), seg)
        y = y * lax.rsqrt(msq + EPS) * qkw_ref[part:part + 1, c:c + cw]
        una_ref[:, part * NA_W + c:part * NA_W + c + cw] = y.astype(BF16)


def _inproj(x, nw, w, qkw, seg):
    pair = isinstance(x, tuple)
    if pair:
        xa, xb = x
        d = xa.shape[1]
        na_blocks = xa.shape[0] // TM
        n = xa.shape[0] + xb.shape[0]
        x_specs = [pl.BlockSpec((TM, d), lambda i: (jnp.minimum(i, na_blocks - 1), 0)),
                   pl.BlockSpec((TM, d), lambda i: (jnp.maximum(i - na_blocks, 0), 0))]
        x_args = [xa, xb]
    else:
        n, d = x.shape
        na_blocks = None
        x_specs = [pl.BlockSpec((TM, d), lambda i: (i, 0))]
        x_args = [x]
    n_hg = w.shape[1] - 3 * NA_W
    tok = lambda width: pl.BlockSpec((TM, width), lambda i: (i, 0))
    out_specs = [tok(n_hg), tok(3 * NA_W)]
    out_shape = [jax.ShapeDtypeStruct((n, n_hg), F32), jax.ShapeDtypeStruct((n, 3 * NA_W), BF16)]
    if pair:
        out_specs.append(tok(d))
        out_shape.append(jax.ShapeDtypeStruct((n, d), F32))
    return pl.pallas_call(
        functools.partial(_inproj_body, first_blocks=na_blocks),
        grid=(n // TM,),
        in_specs=x_specs + [_resident(nw.shape), _resident(w.shape), _resident(qkw.shape), _resident(seg.shape)],
        out_specs=out_specs,
        out_shape=out_shape,
        compiler_params=_params("arbitrary"),
        name="inproj_pair" if pair else "inproj",
    )(*x_args, nw, w, qkw, seg)


def _split2(x):
    hi = x.astype(BF16)
    lo = (x - hi.astype(F32)).astype(BF16)
    return hi, lo


def _hgrn_prep(q_ref, f_ref, lb_ref, tri_ref, qs_ref, ks_ref, bs_ref, r0, reverse):
    c = HG_C
    rows = pl.ds(r0, c)
    q = q_ref[rows, :]
    x = f_ref[rows, :]
    la = lb_ref[0:1, :]
    l1 = lb_ref[1:2, :]
    ls = jnp.minimum(x, 0.0) - jnp.log(1.0 + jnp.exp(-jnp.abs(x)))
    cc = l1 + ls
    g = jnp.maximum(la, cc) + jnp.log(1.0 + jnp.exp(-jnp.abs(la - cc)))
    bc = _dot(tri_ref[...], jnp.concatenate(_split2(g), axis=1))
    b = bc[:, :HG_W] + bc[:, HG_W:]
    qs = q / (1.0 + jnp.exp(-q))
    qs_ref[rows, :] = qs
    ks_ref[rows, :] = jnp.exp(cc - x)
    bs_ref[rows, :] = b
    m = c // 2
    if reverse:
        near, total = b[m:m + 1, :], b[0:1, :]
    else:
        near, total = b[m - 1:m, :], b[c - 1:c, :]
    qmax = jnp.max(jnp.abs(qs), axis=0, keepdims=True)
    return jnp.maximum(-near, near - total) + jnp.log(jnp.maximum(qmax, 1.0))


def _hgrn_chunk(q_ref, k_ref, b_ref, v_ref, o_ref, st_ref, r0, reverse, fast):
    c = HG_C
    rows = pl.ds(r0, c)
    q = q_ref[rows, :]
    k = k_ref[rows, :]
    b = b_ref[rows, :]
    v = v_ref[rows, :]

    row = lax.broadcasted_iota(jnp.int32, (c, HG_D), 0)
    ri = lax.broadcasted_iota(jnp.int32, (c, c), 0)
    ci = lax.broadcasted_iota(jnp.int32, (c, c), 1)
    nb = c // 8
    i8 = lax.broadcasted_iota(jnp.int32, (nb, 8, HG_D), 1)

    for h in range(HG_HEADS):
        sl = slice(h * HG_D, (h + 1) * HG_D)
        bh, qh, kh, vh = b[:, sl], q[:, sl], k[:, sl], v[:, sl]
        vb = vh.astype(BF16)
        st = st_ref[h]
        btot = bh[0:1, :] if reverse else bh[c - 1:c, :]
        qe = (qh * jnp.exp(bh)).astype(BF16)
        ke = (kh * jnp.exp(btot - bh)).astype(BF16)
        o = _dot_nt(qe, st.astype(BF16))
        st_ref[h] = st * jnp.exp(btot) + _dot_tn(vb, ke)

        if fast:
            m = c // 2
            r = bh[m:m + 1, :] if reverse else bh[m - 1:m, :]
            qr = (qh * jnp.exp(bh - r)).astype(BF16)
            kr = (kh * jnp.exp(r - bh)).astype(BF16)
            causal = (ri <= ci) if reverse else (ri >= ci)
            a = jnp.where(causal, _dot_nt(qr, kr), 0.0)
            o_ref[rows, sl] = o + _dot(a.astype(BF16), vb)
            continue

        a = jnp.zeros((c, c), F32)
        w = c // 2
        while w >= 8:
            grp = c // (2 * w)
            b4 = bh.reshape(grp, 2 * w, HG_D)
            edge = w if reverse else w - 1
            r = jnp.broadcast_to(b4[:, edge:edge + 1, :], (grp, 2 * w, HG_D)).reshape(c, HG_D)
            e = jnp.exp(-jnp.abs(bh - r))
            late = (row & (2 * w - 1)) >= w
            is_q = jnp.logical_not(late) if reverse else late
            qs = jnp.where(is_q, qh * e, 0.0).astype(BF16)
            ks = jnp.where(is_q, 0.0, kh * e).astype(BF16)
            aw = _dot_nt(qs, ks)
            if grp > 1:
                sh = int(np.log2(2 * w))
                aw = jnp.where((ri >> sh) == (ci >> sh), aw, 0.0)
            a = a + aw
            w //= 2
        o = o + _dot(a.astype(BF16), vb)

        b3, q3, k3, v3 = (t.reshape(nb, 8, HG_D) for t in (bh, qh, kh, vh))
        od = jnp.zeros((nb, 8, HG_D), F32)
        for j in range(8):
            valid = (i8 <= j) if reverse else (i8 >= j)
            e = jnp.exp(jnp.where(valid, b3 - b3[:, j:j + 1, :], NEG))
            aj = jnp.sum(q3 * e * k3[:, j:j + 1, :], axis=-1, keepdims=True)
            od = od + aj * v3[:, j:j + 1, :]
        o_ref[rows, sl] = o + od.reshape(c, HG_D)


def _hgrn_body(fwd_ref, bwd_ref, first_ref, qf_ref, ff_ref, vf_ref, qb_ref, fb_ref, vb_ref,
               lbf_ref, lbb_ref, tril_ref, triu_ref, of_ref, ob_ref,
               stf_ref, stb_ref, qsf_ref, ksf_ref, bsf_ref, qsb_ref, ksb_ref, bsb_ref):
    del fwd_ref, bwd_ref
    g = pl.program_id(0)

    @pl.when(first_ref[g] == 1)
    def _():
        stf_ref[...] = jnp.zeros_like(stf_ref)
        stb_ref[...] = jnp.zeros_like(stb_ref)

    n_chunks = HG_TB // HG_C

    def prep(ci, worst):
        r0 = pl.multiple_of(ci * HG_C, HG_C)
        df = _hgrn_prep(qf_ref, ff_ref, lbf_ref, tril_ref, qsf_ref, ksf_ref, bsf_ref, r0, False)
        db = _hgrn_prep(qb_ref, fb_ref, lbb_ref, triu_ref, qsb_ref, ksb_ref, bsb_ref, r0, True)
        return jnp.maximum(worst, jnp.maximum(df, db))

    worst = jnp.max(lax.fori_loop(0, n_chunks, prep, jnp.zeros((1, HG_W), F32)))
    fits = worst <= HG_MAX_EXPONENT

    def run(fast):
        def body(ci, carry):
            rf = pl.multiple_of(ci * HG_C, HG_C)
            _hgrn_chunk(qsf_ref, ksf_ref, bsf_ref, vf_ref, of_ref, stf_ref, rf, False, fast)
            rb = pl.multiple_of((n_chunks - 1 - ci) * HG_C, HG_C)
            _hgrn_chunk(qsb_ref, ksb_ref, bsb_ref, vb_ref, ob_ref, stb_ref, rb, True, fast)
            return carry

        lax.fori_loop(0, n_chunks, body, 0, unroll=2 if fast else 1)

    @pl.when(fits)
    def _():
        run(True)

    @pl.when(jnp.logical_not(fits))
    def _():
        run(False)


def _hgrn(uhg, lbf, lbb, seq_lens):
    n = uhg.shape[0]
    fwd, bwd, first = [], [], []
    base = 0
    for t in seq_lens:
        nt = t // HG_TB
        for j in range(nt):
            fwd.append(base + j)
            bwd.append(base + nt - 1 - j)
            first.append(1 if j == 0 else 0)
        base += nt
    fwd, bwd, first = (jnp.asarray(np.array(a, np.int32)) for a in (fwd, bwd, first))
    tril = jnp.asarray(np.tril(np.ones((HG_C, HG_C), np.float32)), BF16)
    triu = jnp.asarray(np.triu(np.ones((HG_C, HG_C), np.float32)), BF16)

    def spec(which, col):
        if which == 0:
            return pl.BlockSpec((HG_TB, HG_W), lambda g, f, b, s: (f[g], col))
        return pl.BlockSpec((HG_TB, HG_W), lambda g, f, b, s: (b[g], col))

    const = lambda shape: pl.BlockSpec(shape, lambda g, f, b, s: (0, 0))
    grid_spec = pltpu.PrefetchScalarGridSpec(
        num_scalar_prefetch=3,
        grid=(n // HG_TB,),
        in_specs=[spec(0, 0), spec(0, 1), spec(0, 3), spec(1, 0), spec(1, 2), spec(1, 3),
                  const(lbf.shape), const(lbb.shape), const(tril.shape), const(triu.shape)],
        out_specs=[spec(0, 0), spec(1, 0)],
        scratch_shapes=[pltpu.VMEM((HG_HEADS, HG_D, HG_D), F32)] * 2 + [pltpu.VMEM((HG_TB, HG_W), F32)] * 6,
    )
    return pl.pallas_call(
        _hgrn_body,
        grid_spec=grid_spec,
        out_shape=[jax.ShapeDtypeStruct((n, HG_W), F32), jax.ShapeDtypeStruct((n, HG_W), F32)],
        compiler_params=_params("arbitrary"),
        name="hgrn",
    )(fwd, bwd, first, uhg, uhg, uhg, uhg, uhg, uhg, lbf, lbb, tril, triu)


def _na_body(prev_ref, next_ref, rloc_ref, rtot_ref, q_ref, kp_ref, kc_ref, kn_ref,
             vp_ref, vc_ref, vn_ref, bias_ref, o_ref, ks_ref, vs_ref):
    del prev_ref, next_ref
    g = pl.program_id(1)
    blk = q_ref.shape[0]
    rb = blk // GRID_W
    ks_ref[0:blk, :] = kp_ref[...]
    ks_ref[blk:2 * blk, :] = kc_ref[...]
    ks_ref[2 * blk:3 * blk, :] = kn_ref[...]
    vs_ref[0:blk, :] = vp_ref[...]
    vs_ref[blk:2 * blk, :] = vc_ref[...]
    vs_ref[2 * blk:3 * blk, :] = vn_ref[...]
    r0 = rloc_ref[g]
    n_rows = rtot_ref[g]
    first_head = lax.broadcasted_iota(jnp.int32, (GRID_W, LANES), 1) < NA_DH
    nkeys = NA_UNION * GRID_W
    qrows = 2 * GRID_W

    for i0 in range(0, rb, NA_QUAD):
        rs0 = jnp.clip(r0 + i0 - WIN_H // 2, 0, n_rows - WIN_H)
        off = pl.multiple_of((rs0 - r0 + rb) * GRID_W, GRID_W)
        kw = ks_ref[pl.ds(off, nkeys), :]
        vw = vs_ref[pl.ds(off, nkeys), :]
        pieces = []
        for a in range(NA_QUAD):
            q2 = q_ref[(i0 + a) * GRID_W:(i0 + a + 1) * GRID_W, :]
            zero = jnp.zeros_like(q2)
            pieces += [jnp.where(first_head, q2, zero), jnp.where(first_head, zero, q2)]
        s = _dot_nt(jnp.concatenate(pieces, axis=0), kw)
        probs, sums = [], []
        for a in range(NA_QUAD):
            r = r0 + i0 + a
            rs = jnp.clip(r - WIN_H // 2, 0, n_rows - WIN_H)
            rel = rs - rs0
            entry = jnp.where(rel > 0, WIN_H - 1 + rel, rs - r + WIN_H - 1)
            sa = s[a * qrows:(a + 1) * qrows] + bias_ref[0, entry]
            p = jnp.exp(sa - jnp.max(sa, axis=-1, keepdims=True))
            sums.append(jnp.sum(p, axis=-1, keepdims=True))
            probs.append(p.astype(BF16))
        ov = _dot(jnp.concatenate(probs, axis=0), vw)
        for a in range(NA_QUAD):
            oa = ov[a * qrows:(a + 1) * qrows] / sums[a]
            out = jnp.where(first_head, oa[:GRID_W], oa[GRID_W:])
            o_ref[(i0 + a) * GRID_W:(i0 + a + 1) * GRID_W, :] = out.astype(o_ref.dtype)


def _natten(una, bias, seq_lens):
    n = una.shape[0]
    rb = max(c for c in NA_RB_CHOICES if all((t // GRID_W) % c == 0 for t in seq_lens))
    blk = rb * GRID_W
    prev, nxt, rloc, rtot = [], [], [], []
    base = 0
    for t in seq_lens:
        rows = t // GRID_W
        nb = rows // rb
        for j in range(nb):
            prev.append(base + max(j - 1, 0))
            nxt.append(base + min(j + 1, nb - 1))
            rloc.append(j * rb)
            rtot.append(rows)
        base += nb
    prev, nxt, rloc, rtot = (jnp.asarray(np.array(a, np.int32)) for a in (prev, nxt, rloc, rtot))
    npair = NA_W // LANES

    def cur(part):
        return pl.BlockSpec((blk, LANES), lambda p, g, pv, nx, rl, rt: (g, part * npair + p))

    def nbr(part, which):
        if which == 0:
            return pl.BlockSpec((blk, LANES), lambda p, g, pv, nx, rl, rt: (pv[g], part * npair + p))
        return pl.BlockSpec((blk, LANES), lambda p, g, pv, nx, rl, rt: (nx[g], part * npair + p))

    grid_spec = pltpu.PrefetchScalarGridSpec(
        num_scalar_prefetch=4,
        grid=(npair, n // blk),
        in_specs=[cur(0), nbr(1, 0), cur(1), nbr(1, 1), nbr(2, 0), cur(2), nbr(2, 1),
                  pl.BlockSpec((1,) + bias.shape[1:], lambda p, g, pv, nx, rl, rt: (p, 0, 0, 0))],
        out_specs=pl.BlockSpec((blk, LANES), lambda p, g, pv, nx, rl, rt: (g, p)),
        scratch_shapes=[pltpu.VMEM((3 * blk, LANES), BF16), pltpu.VMEM((3 * blk, LANES), BF16)],
    )
    return pl.pallas_call(
        _na_body,
        grid_spec=grid_spec,
        out_shape=jax.ShapeDtypeStruct((n, NA_W), BF16),
        compiler_params=_params("arbitrary", "arbitrary"),
        name="natten",
    )(prev, nxt, rloc, rtot, una, una, una, una, una, una, una, bias)


def _na_bias_table(rpb):
    c = np.arange(GRID_W)
    cstart = np.clip(c - WIN_W // 2, 0, GRID_W - WIN_W)
    kc = np.arange(GRID_W)
    mask = (kc[None, :] >= cstart[:, None]) & (kc[None, :] < cstart[:, None] + WIN_W)
    dx = np.clip(kc[None, :] - c[:, None] + WIN_W - 1, 0, 2 * WIN_W - 2)
    onehot = np.zeros((2 * WIN_W - 1, GRID_W * GRID_W), np.float32)
    onehot[dx.reshape(-1), np.arange(GRID_W * GRID_W)] = mask.reshape(-1)
    t = jnp.einsum('hyx,xq->hyq', rpb.astype(F32), jnp.asarray(onehot), precision=lax.Precision.HIGHEST)
    t = t + jnp.asarray(np.where(mask.reshape(-1), 0.0, NEG).astype(np.float32))
    def entry(o, rel):
        neg = lambda k: jnp.full((NA_HEADS, k, GRID_W * GRID_W), NEG, F32)
        return jnp.concatenate([neg(rel), t[:, o:o + WIN_H], neg(NA_UNION - WIN_H - rel)], axis=1)

    t = jnp.stack([entry(o, 0) for o in range(WIN_H)]
                  + [entry(WIN_H // 2 - 1, rel) for rel in range(1, NA_QUAD)], axis=1)
    ne = t.shape[1]
    t = t.reshape(NA_HEADS, ne, NA_UNION, GRID_W, GRID_W)
    t = t.transpose(0, 1, 3, 2, 4).reshape(NA_HEADS, ne, GRID_W, NA_UNION * GRID_W)
    t = t.reshape(NA_HEADS // 2, 2, ne, GRID_W, NA_UNION * GRID_W).transpose(0, 2, 1, 3, 4)
    return t.reshape(NA_HEADS // 2, ne, 2 * GRID_W, NA_UNION * GRID_W)


def _route(x, nw_ref, whi_ref, wlo_ref, ids_ref, wts_ref):
    ms = jnp.mean(x * x, axis=-1, keepdims=True)
    h = x * lax.rsqrt(ms + EPS) * nw_ref[...]
    hh = h.astype(BF16)
    hl = (h - hh.astype(F32)).astype(BF16)
    logits = _dot(hh, whi_ref[...]) + (_dot(hl, whi_ref[...]) + _dot(hh, wlo_ref[...]))
    col = lax.broadcasted_iota(jnp.int32, logits.shape, 1).astype(F32)
    logits = jnp.where(col < N_EXPERTS, logits, -jnp.inf)
    m1 = jnp.max(logits, axis=-1, keepdims=True)
    i1 = jnp.min(jnp.where(logits == m1, col, float(LANES)), axis=-1, keepdims=True)
    rest = jnp.where(col == i1, -jnp.inf, logits)
    m2 = jnp.max(rest, axis=-1, keepdims=True)
    i2 = jnp.min(jnp.where(rest == m2, col, float(LANES)), axis=-1, keepdims=True)
    e = jnp.exp(m2 - m1)
    w1 = 1.0 / (1.0 + e)
    w2 = e / (1.0 + e)
    ids_ref[...] = jnp.where(col == 0.0, i1, jnp.where(col == 1.0, i2, 0.0)).astype(jnp.int32)
    wts_ref[...] = jnp.where(col == 0.0, w1, jnp.where(col == 1.0, w2, 0.0))


def _outproj_body(of_ref, ob_ref, g_ref, ona_ref, x_ref, gw_ref, w_ref, *rest):
    xo_ref = rest[-1] if len(rest) == 1 else rest[3]
    o = of_ref[...] + ob_ref[...]
    g = g_ref[...]
    gate = g / (1.0 + jnp.exp(-g))
    parts = []
    for h in range(HG_HEADS):
        sl = slice(h * HG_D, (h + 1) * HG_D)
        oh = o[:, sl]
        ms = jnp.mean(oh * oh, axis=-1, keepdims=True)
        parts.append(oh * lax.rsqrt(ms + EPS) * gw_ref[...] * gate[:, sl])
    hg = jnp.concatenate(parts, axis=1).astype(BF16)
    y = _dot(hg, w_ref[0:HG_W, :]) + _dot(ona_ref[...], w_ref[HG_W:, :])
    x = x_ref[...] + y
    xo_ref[...] = x
    if len(rest) > 1:
        nw_ref, whi_ref, wlo_ref, _, ids_ref, wts_ref = rest
        _route(x, nw_ref, whi_ref, wlo_ref, ids_ref, wts_ref)


def _outproj(o_f, o_b, uhg, o_na, x, gw, w, route=None):
    n, d = x.shape
    tok = lambda width, col=0: pl.BlockSpec((TM, width), lambda i: (i, col))
    in_specs = [tok(HG_W), tok(HG_W), tok(HG_W, 4), tok(NA_W), tok(d), _resident(gw.shape), _resident(w.shape)]
    out_specs = [tok(d)]
    out_shape = [jax.ShapeDtypeStruct((n, d), F32)]
    args = [o_f, o_b, uhg, o_na, x, gw, w]
    if route is not None:
        in_specs += [_resident(a.shape) for a in route]
        args += list(route)
        out_specs += [tok(LANES), tok(LANES)]
        out_shape += [jax.ShapeDtypeStruct((n, LANES), jnp.int32), jax.ShapeDtypeStruct((n, LANES), F32)]
    out = pl.pallas_call(
        _outproj_body,
        grid=(n // TM,),
        in_specs=in_specs,
        out_specs=out_specs,
        out_shape=out_shape,
        compiler_params=_params("arbitrary"),
        name="outproj_route" if route is not None else "outproj",
    )(*args)
    return out[0] if route is None else out


def _ffn_body(te_ref, nv_ref, x_ref, nw_ref, wg_ref, wu_ref, wd_ref, o_ref, xn_ref, *, residual):
    del te_ref
    i = pl.program_id(0)
    f = pl.program_id(1)
    valid = i < nv_ref[0]

    @pl.when(jnp.logical_and(f == 0, valid))
    def _():
        xr = x_ref[...]
        ms = jnp.mean(xr * xr, axis=-1, keepdims=True)
        xn_ref[...] = (xr * lax.rsqrt(ms + EPS) * nw_ref[...]).astype(BF16)
        o_ref[...] = xr if residual else jnp.zeros_like(xr)

    @pl.when(jnp.logical_and(f == 0, jnp.logical_not(valid)))
    def _():
        o_ref[...] = jnp.zeros_like(o_ref)

    @pl.when(valid)
    def _():
        x = xn_ref[...]
        tf = wg_ref.shape[2]
        for c0 in range(0, tf, FFN_SUB):
            c1 = min(c0 + FFN_SUB, tf)
            gate = _dot(x, wg_ref[0, :, c0:c1])
            up = _dot(x, wu_ref[0, :, c0:c1])
            hid = (gate / (1.0 + jnp.exp(-gate)) * up).astype(BF16)
            o_ref[...] += _dot(hid, wd_ref[0, c0:c1, :])


def _ffn(x, nw, wg, wu, wd, tile_expert, n_valid, tm, tf, residual):
    p, d = x.shape
    ff = wg.shape[2]
    assert ff % tf == 0 and tf % MXU_N == 0
    once = dict(pipeline_mode=pl.Buffered(1)) if (wg.shape[0] == 1 and tf == ff) else {}
    row = pl.BlockSpec((tm, d), lambda i, f, te, nv: (i, 0))
    grid_spec = pltpu.PrefetchScalarGridSpec(
        num_scalar_prefetch=2,
        grid=(p // tm, ff // tf),
        in_specs=[row,
                  pl.BlockSpec(nw.shape, lambda i, f, te, nv: (0, 0)),
                  pl.BlockSpec((1, d, tf), lambda i, f, te, nv: (te[i], 0, f), **once),
                  pl.BlockSpec((1, d, tf), lambda i, f, te, nv: (te[i], 0, f), **once),
                  pl.BlockSpec((1, tf, d), lambda i, f, te, nv: (te[i], f, 0), **once)],
        out_specs=row,
        scratch_shapes=[pltpu.VMEM((tm, d), BF16)],
    )
    return pl.pallas_call(
        functools.partial(_ffn_body, residual=residual),
        grid_spec=grid_spec,
        out_shape=jax.ShapeDtypeStruct((p, d), F32),
        compiler_params=pltpu.CompilerParams(dimension_semantics=("arbitrary", "arbitrary"),
                                             vmem_limit_bytes=FFN_VMEM_LIMIT),
        name="ffn_res" if residual else "ffn_grouped",
    )(tile_expert, n_valid, x, nw, wg, wu, wd)


def _row_copy(src_ref, dst_ref, sem, src_row, dst_row):
    return pltpu.make_async_copy(src_ref.at[pl.ds(src_row, 1)], dst_ref.at[pl.ds(dst_row, 1)], sem)


def _gather_into(idx_smem, src_hbm, dst_ref, sem):
    def start(b, carry):
        for u in range(DMA_UNROLL):
            r = b * DMA_UNROLL + u
            _row_copy(src_hbm, dst_ref, sem, idx_smem[r], r).start(priority=u % 2)
        return carry

    lax.fori_loop(0, GATHER_ROWS // DMA_UNROLL, start, 0)

    def wait(b, carry):
        for u in range(DMA_UNROLL):
            _row_copy(src_hbm, dst_ref, sem, 0, b * DMA_UNROLL + u).wait()
        return carry

    lax.fori_loop(0, GATHER_ROWS // DMA_UNROLL, wait, 0)


def _gather_body(idx_hbm, src_hbm, o_ref, idx_smem, sem_idx, sem_rows):
    i = pl.program_id(0)
    cp = pltpu.make_async_copy(idx_hbm.at[pl.ds(i * GATHER_ROWS, GATHER_ROWS)], idx_smem, sem_idx)
    cp.start()
    cp.wait()
    _gather_into(idx_smem, src_hbm, o_ref, sem_rows)


def _gather_rows(idx, src):
    p = idx.shape[0]
    w = src.shape[1]
    return pl.pallas_call(
        _gather_body,
        grid=(p // GATHER_ROWS,),
        in_specs=[pl.BlockSpec(memory_space=pl.ANY), pl.BlockSpec(memory_space=pl.ANY)],
        out_specs=pl.BlockSpec((GATHER_ROWS, w), lambda i: (i, 0)),
        out_shape=jax.ShapeDtypeStruct((p, w), src.dtype),
        scratch_shapes=[pltpu.SMEM((GATHER_ROWS,), jnp.int32),
                        pltpu.SemaphoreType.DMA(()), pltpu.SemaphoreType.DMA(())],
        compiler_params=_params("arbitrary"),
        name="gather_rows",
    )(idx, src)


def _combine_body(pos_hbm, y_hbm, x_ref, wts_ref, o_ref, pos_smem, buf_ref, sem_idx, sem_rows):
    i = pl.program_id(0)
    tc = GATHER_ROWS // TOP_K
    cp = pltpu.make_async_copy(pos_hbm.at[pl.ds(i * GATHER_ROWS, GATHER_ROWS)], pos_smem, sem_idx)
    cp.start()
    cp.wait()
    _gather_into(pos_smem, y_hbm, buf_ref, sem_rows)
    w = wts_ref[...]
    o_ref[...] = x_ref[...] + (w[:, 0:1] * buf_ref[0:tc, :] + w[:, 1:2] * buf_ref[tc:2 * tc, :])


def _combine(pos, y, x, wts):
    n, d = x.shape
    tc = GATHER_ROWS // TOP_K
    return pl.pallas_call(
        _combine_body,
        grid=(n // tc,),
        in_specs=[pl.BlockSpec(memory_space=pl.ANY), pl.BlockSpec(memory_space=pl.ANY),
                  pl.BlockSpec((tc, d), lambda i: (i, 0)), pl.BlockSpec((tc, LANES), lambda i: (i, 0))],
        out_specs=pl.BlockSpec((tc, d), lambda i: (i, 0)),
        out_shape=jax.ShapeDtypeStruct((n, d), F32),
        scratch_shapes=[pltpu.SMEM((GATHER_ROWS,), jnp.int32), pltpu.VMEM((GATHER_ROWS, d), F32),
                        pltpu.SemaphoreType.DMA(()), pltpu.SemaphoreType.DMA(())],
        compiler_params=_params("arbitrary"),
        name="combine",
    )(pos, y, x, wts)


def _moe(x, ids, wts, nw, wg, wu, wd):
    n, d = x.shape
    e_flat = ids[:, :TOP_K].reshape(-1)
    na = n * TOP_K
    onehot = (e_flat[:, None] == jnp.arange(N_EXPERTS, dtype=jnp.int32)[None, :]).astype(jnp.int32)
    csum = jnp.cumsum(onehot, axis=0)
    counts = csum[-1]
    rank = jnp.sum(csum * onehot, axis=1) - 1
    padded = (counts + FFN_TM - 1) // FFN_TM * FFN_TM
    pad_end = jnp.cumsum(padded)
    pad_off = pad_end - padded
    raw_off = jnp.cumsum(counts) - counts
    pos = jnp.sum(onehot * pad_off[None, :], axis=1) + rank
    n_rows = na + N_EXPERTS * (FFN_TM - 1)
    n_rows = -(-n_rows // GATHER_ROWS) * GATHER_ROWS
    tile_start = jnp.arange(n_rows // FFN_TM, dtype=jnp.int32) * FFN_TM
    tile_expert = jnp.minimum(
        jnp.sum(tile_start[:, None] >= pad_end[None, :], axis=1), N_EXPERTS - 1).astype(jnp.int32)
    n_valid = (pad_end[-1] // FFN_TM).astype(jnp.int32).reshape(1)
    order = jnp.argsort(e_flat, stable=True).astype(jnp.int32)
    row = jnp.arange(n_rows, dtype=jnp.int32)
    e_row = jnp.repeat(tile_expert, FFN_TM)
    local = row - pad_off[e_row]
    src = jnp.clip(raw_off[e_row] + local, 0, na - 1)
    src_tok = jnp.where(local < counts[e_row], order[src] // TOP_K, 0).astype(jnp.int32)

    xs = _gather_rows(src_tok, x)
    y = _ffn(xs, nw, wg, wu, wd, tile_expert, n_valid, tm=FFN_TM, tf=wg.shape[2] // 2, residual=False)
    tc = GATHER_ROWS // TOP_K
    pos_tiles = pos.astype(jnp.int32).reshape(n // tc, tc, TOP_K).transpose(0, 2, 1).reshape(-1)
    return _combine(pos_tiles, y, x, wts)


def kernel(x_prompt, x_sample, norm_mix, w_in, lb_logits, hg_gnorm, na_qnorm, na_knorm, na_rpb, w_out, norm_ffn,
           ffn_gate, ffn_up, ffn_down, router, exp_gate, exp_up, exp_down):
    bp, tp, d = x_prompt.shape
    bs, ts, _ = x_sample.shape
    depth = w_in.shape[0]
    seq_lens = [tp] * bp + [ts] * bs
    assert tp % HG_TB == 0 and ts % HG_TB == 0 and min(tp, ts) // GRID_W >= WIN_H
    n = bp * tp + bs * ts
    assert n % FFN_TM == 0 and GATHER_ROWS % FFN_TM == 0 and n % (GATHER_ROWS // TOP_K) == 0
    x = (x_prompt.reshape(bp * tp, d), x_sample.reshape(bs * ts, d))

    lb = jnp.cumsum(jax.nn.softmax(lb_logits.astype(F32), axis=1), axis=1)
    lb = lb - lb[:, :1]
    lbp = jnp.stack([jnp.log(lb), jnp.log1p(-lb)], axis=2)

    seg = jnp.asarray(np.kron(np.eye(MXU_N // NA_DH), np.ones((NA_DH, NA_DH))) / NA_DH, BF16)
    scale = NA_DH ** -0.5

    for l in range(depth):
        qkw = jnp.stack([jnp.tile(na_qnorm[l].astype(F32), NA_HEADS) * scale,
                         jnp.tile(na_knorm[l].astype(F32), NA_HEADS)])
        res = _inproj(x, norm_mix[l].reshape(1, d), w_in[l].astype(BF16), qkw, seg)
        uhg, una = res[0], res[1]
        if l == 0:
            x = res[2]
        o_f, o_b = _hgrn(uhg, lbp[0, l], lbp[1, l], seq_lens)
        o_na = _natten(una, _na_bias_table(na_rpb[l]), seq_lens)
        nw = norm_ffn[l].reshape(1, d)
        j = l // 2
        gw = hg_gnorm[l].reshape(1, HG_D)
        if l % 2 == 0:
            x = _outproj(o_f, o_b, uhg, o_na, x, gw, w_out[l].astype(BF16))
            n_tiles = n // FFN_TM
            x = _ffn(x, nw, ffn_gate[j:j + 1].astype(BF16), ffn_up[j:j + 1].astype(BF16),
                     ffn_down[j:j + 1].astype(BF16), jnp.zeros((n_tiles,), jnp.int32),
                     jnp.full((1,), n_tiles, jnp.int32), tm=FFN_TM, tf=ffn_gate.shape[2], residual=True)
        else:
            wr = jnp.pad(router[j].astype(F32), ((0, 0), (0, LANES - N_EXPERTS)))
            whi = wr.astype(BF16)
            wlo = (wr - whi.astype(F32)).astype(BF16)
            x, ids, wts = _outproj(o_f, o_b, uhg, o_na, x, gw, w_out[l].astype(BF16), route=(nw, whi, wlo))
            x = _moe(x, ids, wts, nw, exp_gate[j].astype(BF16), exp_up[j].astype(BF16), exp_down[j].astype(BF16))

    y_prompt = x[:bp * tp].reshape(bp, tp, d)
    y_sample = x[bp * tp:].reshape(bs, ts, d)
    return (y_prompt, y_sample)
```
